```python
import math
import jax
import jax.numpy as jnp
from jax import lax
import numpy as np

D_MODEL = 1024
BATCH = 2
SEQ = 8192
DEPTH = 2
DEC_BATCH = 128
DEC_SEQ = 4
PAST_LEN = 2048
PAGE_SIZE = 128

A_HEADS = 8
A_QK_DIM = 64
A_V_DIM = 2 * A_QK_DIM
A_QK_COLS = A_HEADS * 2 * A_QK_DIM
A_WIDTH = A_HEADS * A_V_DIM
B_HEADS = 8
B_DK = 128
B_DV = 128
B_KEY_WIDTH = B_HEADS * B_DK
B_WIDTH = B_HEADS * B_DV
B_CONV_CH = 2 * B_KEY_WIDTH + B_WIDTH
CONV_W = 4
GDN_CHUNK = 64
Q_BLOCK = 128
IN_SIZES = (A_QK_COLS, A_QK_COLS, A_WIDTH, A_WIDTH, B_CONV_CH, B_WIDTH, B_HEADS, B_HEADS, D_MODEL, D_MODEL)
N_IN = sum(IN_SIZES)
DEEPNORM_ALPHA = (2 * DEPTH) ** 0.25
DEEPNORM_BETA = (8 * DEPTH) ** -0.25
LN_EPS = 1e-5
RMS_EPS = 1e-6
POOL_NUM = 5
POOL_DEN = 4

kernel_name = 'hybrid_diffattn_gdn_step'


def split_columns(p):
    idx = [int(i) for i in np.cumsum(IN_SIZES)[:-1]]
    return jnp.split(p, idx, axis=-1)


def layer_norm(x, g, b):
    xf = x.astype(jnp.float32)
    xc = xf - jnp.mean(xf, -1, keepdims=True)
    var = jnp.mean(xc * xc, -1, keepdims=True)
    return (xc * lax.rsqrt(var + LN_EPS) * g.astype(jnp.float32) + b.astype(jnp.float32)).astype(x.dtype)


def rms_norm(x, w):
    xf = x.astype(jnp.float32)
    return xf * lax.rsqrt(jnp.mean(xf * xf, -1, keepdims=True) + RMS_EPS) * w.astype(jnp.float32)


def l2_normalize(x):
    return x * lax.rsqrt(jnp.sum(x * x, -1, keepdims=True) + RMS_EPS)


def alibi_slopes():
    return 2.0 ** (-8.0 * jnp.arange(1, A_HEADS + 1, dtype=jnp.float32) / A_HEADS)


def diff_attend(q, k, v, q_pos, k_pos, lam):
    s = jnp.einsum('bqhmd,bkhmd->bhmqk', q, k, preferred_element_type=jnp.float32) * (A_QK_DIM ** -0.5)
    dist = q_pos[:, None] - k_pos[None, :]
    s = s - alibi_slopes()[None, :, None, None, None] * dist.astype(jnp.float32)
    s = jnp.where(dist >= 0, s, -jnp.inf)
    p = jax.nn.softmax(s, axis=-1)
    p = p[:, :, 0] - lam * p[:, :, 1]
    return jnp.einsum('bhqk,bkhv->bqhv', p.astype(v.dtype), v)


def diff_attn_prompt(q, k, v, lam):
    bn, t = q.shape[:2]
    nb = t // Q_BLOCK
    qb = jnp.moveaxis(q.reshape((bn, nb, Q_BLOCK) + q.shape[2:]), 1, 0)
    k_pos = jnp.arange(t)

    def one_block(args):
        q_blk, start = args
        return diff_attend(q_blk, k, v, start + jnp.arange(Q_BLOCK), k_pos, lam)

    o = lax.map(one_block, (qb, jnp.arange(nb) * Q_BLOCK))
    return jnp.moveaxis(o, 0, 1).reshape(bn, t, A_HEADS, A_V_DIM)


def diff_attn_sample(q, k, v, k_past, v_past, lam):
    t = q.shape[1]
    past = k_past.shape[1]
    k_all = jnp.concatenate([k_past.astype(k.dtype), k], axis=1)
    v_all = jnp.concatenate([v_past.astype(v.dtype), v], axis=1)
    return diff_attend(q, k_all, v_all, past + jnp.arange(t), jnp.arange(past + t), lam)


def gated_delta_chunked(q, k, v, g, beta, s0):
    bn, t, h, dk = q.shape
    dv = v.shape[-1]
    c = min(GDN_CHUNK, t)
    pad = (-t) % c
    if pad:
        def padt(a):
            return jnp.pad(a, [(0, 0), (0, pad)] + [(0, 0)] * (a.ndim - 2))
        q, k, v, g, beta = padt(q), padt(k), padt(v), padt(g), padt(beta)
    n = (t + pad) // c

    def chunks(a):
        a = a.reshape((bn, n, c) + a.shape[2:])
        return jnp.moveaxis(a, (1, 3), (0, 2))

    qc, kc, vc, gc, bc = chunks(q), chunks(k), chunks(v), chunks(g), chunks(beta)
    gcum = jnp.cumsum(gc, axis=-1)
    qc = qc * (dk ** -0.5)
    idx = jnp.arange(c)
    causal = idx[:, None] >= idx[None, :]
    strict = idx[:, None] > idx[None, :]
    decay = jnp.exp(jnp.where(causal, gcum[..., :, None] - gcum[..., None, :], -jnp.inf))
    kb = kc * bc[..., None]
    kk = jnp.einsum('nbhik,nbhjk->nbhij', kb, kc) * decay
    m = jnp.eye(c, dtype=jnp.float32) + jnp.where(strict, kk, 0.0)
    rhs = jnp.concatenate([vc * bc[..., None], kb * jnp.exp(gcum)[..., None]], axis=-1)
    sol = lax.linalg.triangular_solve(m, rhs, left_side=True, lower=True, unit_diagonal=True)
    u, w = sol[..., :dv], sol[..., dv:]
    qk = jnp.einsum('nbhik,nbhjk->nbhij', qc, kc) * decay
    q_dec = qc * jnp.exp(gcum)[..., None]
    g_last = gcum[..., -1]
    k_dec = kc * jnp.exp(g_last[..., None] - gcum)[..., None]

    def step(s, inp):
        u_i, w_i, qk_i, qd_i, kd_i, gl_i = inp
        v_new = u_i - jnp.einsum('bhck,bhkv->bhcv', w_i, s)
        o = jnp.einsum('bhck,bhkv->bhcv', qd_i, s) + jnp.einsum('bhij,bhjv->bhiv', qk_i, v_new)
        s = s * jnp.exp(gl_i)[..., None, None] + jnp.einsum('bhck,bhcv->bhkv', kd_i, v_new)
        return s, o

    s_final, o = lax.scan(step, s0, (u, w, qk, q_dec, k_dec, g_last))
    o = jnp.transpose(o, (1, 0, 3, 2, 4)).reshape(bn, n * c, h, dv)[:, :t]
    return o, s_final


def gdn_branch(qkv_pre, z, b_raw, a_raw, conv_buf, s0, conv_w, a_log, dt_bias, norm_w):
    bn, t = qkv_pre.shape[:2]
    xp = jnp.concatenate([conv_buf.astype(qkv_pre.dtype), qkv_pre], axis=1)
    acc = xp[:, 0:t] * conv_w[0]
    for j in range(1, CONV_W):
        acc = acc + xp[:, j:j + t] * conv_w[j]
    qkv = jax.nn.silu(acc.astype(jnp.float32))
    new_buf = xp[:, t:]
    q, k, v = jnp.split(qkv, [B_KEY_WIDTH, 2 * B_KEY_WIDTH], axis=-1)
    q = l2_normalize(q.reshape(bn, t, B_HEADS, B_DK))
    k = l2_normalize(k.reshape(bn, t, B_HEADS, B_DK))
    v = v.reshape(bn, t, B_HEADS, B_DV)
    beta = jax.nn.sigmoid(b_raw.astype(jnp.float32))
    g = -jnp.exp(a_log.astype(jnp.float32)) * jax.nn.softplus(a_raw.astype(jnp.float32) + dt_bias.astype(jnp.float32))
    o, s_new = gated_delta_chunked(q, k, v, g, beta, s0.astype(jnp.float32))
    o = rms_norm(o, norm_w) * jax.nn.silu(z.astype(jnp.float32).reshape(bn, t, B_HEADS, B_DV))
    return o.reshape(bn, t, B_WIDTH).astype(qkv_pre.dtype), s_new.astype(s0.dtype), new_buf


def decoder_layer(x, c, l, k_past, v_past, conv_buf, s0, w_ada, b_ada, w_in, lambda_q1, lambda_k1, lambda_q2,
                  lambda_k2, subln_a, conv_w, a_log, dt_bias, gdn_norm, w_proj_a, w_proj_b, w_out, ln_g, ln_b):
    bn, t, _ = x.shape
    mod = jax.nn.silu(c) @ w_ada[l] + b_ada[l]
    shift, scale, gate = jnp.split(mod[:, None, :], 3, axis=-1)
    h = x * (1 + scale) + shift
    p = h @ w_in[l]
    a_q, a_k, a_v, a_z, b_qkv, b_z, b_beta, b_a, g_a, g_b = split_columns(p)
    q = a_q.reshape(bn, t, A_HEADS, 2, A_QK_DIM)
    k = a_k.reshape(bn, t, A_HEADS, 2, A_QK_DIM)
    v = a_v.reshape(bn, t, A_HEADS, A_V_DIM)
    lam_init = 0.8 - 0.6 * math.exp(-0.3 * l)
    lam = (jnp.exp(jnp.sum((lambda_q1[l] * lambda_k1[l]).astype(jnp.float32)))
           - jnp.exp(jnp.sum((lambda_q2[l] * lambda_k2[l]).astype(jnp.float32))) + lam_init)
    if k_past is None:
        o_a = diff_attn_prompt(q, k, v, lam)
    else:
        o_a = diff_attn_sample(q, k, v, k_past, v_past, lam)
    o_a = rms_norm(o_a, subln_a[l]) * (1.0 - lam_init)
    o_a = (o_a.reshape(bn, t, A_WIDTH) * jax.nn.silu(a_z.astype(jnp.float32))).astype(x.dtype)
    o_b, s_new, buf_new = gdn_branch(b_qkv, b_z, b_beta, b_a, conv_buf, s0, conv_w[l], a_log[l], dt_bias[l], gdn_norm[l])
    y = jax.nn.sigmoid(g_a) * (o_a @ w_proj_a[l]) + jax.nn.sigmoid(g_b) * (o_b @ w_proj_b[l])
    out = y @ w_out[l]
    x_new = layer_norm(DEEPNORM_ALPHA * x + gate * out, ln_g[l], ln_b[l])
    return x_new, k, v, s_new, buf_new


def setup_inputs(seed: int = 0) -> dict:
    key = jax.random.key(seed)
    ks = jax.random.split(key, 26)
    f32 = jnp.float32
    n_pages = PAST_LEN // PAGE_SIZE
    n_pool = (DEC_BATCH * n_pages * POOL_NUM) // POOL_DEN

    def nrm(k, shape, s=1.0):
        return jax.random.normal(k, shape, f32) * s

    x_prompt = nrm(ks[0], (BATCH, SEQ, D_MODEL))
    x_sample = nrm(ks[1], (DEC_BATCH, DEC_SEQ, D_MODEL))
    c_prompt = nrm(ks[2], (BATCH, D_MODEL))
    c_sample = nrm(ks[3], (DEC_BATCH, D_MODEL))
    cache_k = nrm(ks[4], (DEPTH, n_pool, PAGE_SIZE, A_HEADS, 2, A_QK_DIM))
    cache_v = nrm(ks[5], (DEPTH, n_pool, PAGE_SIZE, A_HEADS, A_V_DIM), DEEPNORM_BETA)
    state_gdn = nrm(ks[6], (DEPTH, DEC_BATCH, B_HEADS, B_DK, B_DV), 0.3)
    state_conv = nrm(ks[7], (DEPTH, DEC_BATCH, CONV_W - 1, B_CONV_CH))
    page_table = jax.random.permutation(ks[8], n_pool)[: DEC_BATCH * n_pages].reshape(DEC_BATCH, n_pages).astype(jnp.int32)
    w_ada = nrm(ks[9], (DEPTH, D_MODEL, 3 * D_MODEL), 0.5 * D_MODEL ** -0.5)
    b_ada = nrm(ks[10], (DEPTH, 3 * D_MODEL), 0.02)
    col_scale = jnp.concatenate([
        jnp.ones((2 * A_QK_COLS,), f32), jnp.full((A_WIDTH,), DEEPNORM_BETA, f32),
        jnp.ones((A_WIDTH + 2 * B_KEY_WIDTH,), f32), jnp.full((B_WIDTH,), DEEPNORM_BETA, f32),
        jnp.ones((B_WIDTH + 2 * B_HEADS + 2 * D_MODEL,), f32)])
    w_in = nrm(ks[11], (DEPTH, D_MODEL, N_IN), D_MODEL ** -0.5) * col_scale
    lambda_q1 = nrm(ks[12], (DEPTH, A_QK_DIM), 0.1)
    lambda_k1 = nrm(ks[13], (DEPTH, A_QK_DIM), 0.1)
    lambda_q2 = nrm(ks[14], (DEPTH, A_QK_DIM), 0.1)
    lambda_k2 = nrm(ks[15], (DEPTH, A_QK_DIM), 0.1)
    subln_a = 1.0 + nrm(ks[16], (DEPTH, A_V_DIM), 0.02)
    conv_w = nrm(ks[17], (DEPTH, CONV_W, B_CONV_CH), CONV_W ** -0.5)
    a_log = jnp.log(jax.random.uniform(ks[18], (DEPTH, B_HEADS), f32, 1.0, 16.0))
    dt = jnp.exp(jax.random.uniform(ks[19], (DEPTH, B_HEADS), f32, math.log(1e-3), math.log(1e-1)))
    dt_bias = dt + jnp.log(-jnp.expm1(-dt))
    gdn_norm = 1.0 + nrm(ks[20], (DEPTH, B_DV), 0.02)
    w_proj_a = nrm(ks[21], (DEPTH, A_WIDTH, D_MODEL), DEEPNORM_BETA * A_WIDTH ** -0.5)
    w_proj_b = nrm(ks[22], (DEPTH, B_WIDTH, D_MODEL), DEEPNORM_BETA * B_WIDTH ** -0.5)
    w_out = nrm(ks[23], (DEPTH, D_MODEL, D_MODEL), DEEPNORM_BETA * D_MODEL ** -0.5)
    ln_g = 1.0 + nrm(ks[24], (DEPTH, D_MODEL), 0.02)
    ln_b = nrm(ks[25], (DEPTH, D_MODEL), 0.02)
    return {'x_prompt': x_prompt, 'x_sample': x_sample, 'c_prompt': c_prompt, 'c_sample': c_sample,
            'cache_k': cache_k, 'cache_v': cache_v, 'state_gdn': state_gdn, 'state_conv': state_conv,
            'page_table': page_table, 'w_ada': w_ada, 'b_ada': b_ada, 'w_in': w_in,
            'lambda_q1': lambda_q1, 'lambda_k1': lambda_k1, 'lambda_q2': lambda_q2, 'lambda_k2': lambda_k2,
            'subln_a': subln_a, 'conv_w': conv_w, 'a_log': a_log, 'dt_bias': dt_bias, 'gdn_norm': gdn_norm,
            'w_proj_a': w_proj_a, 'w_proj_b': w_proj_b, 'w_out': w_out, 'ln_g': ln_g, 'ln_b': ln_b}


def reference(x_prompt, x_sample, c_prompt, c_sample, cache_k, cache_v, state_gdn, state_conv, page_table,
              w_ada, b_ada, w_in, lambda_q1, lambda_k1, lambda_q2, lambda_k2, subln_a, conv_w, a_log, dt_bias,
              gdn_norm, w_proj_a, w_proj_b, w_out, ln_g, ln_b):
    params = (w_ada, b_ada, w_in, lambda_q1, lambda_k1, lambda_q2, lambda_k2, subln_a, conv_w, a_log, dt_bias,
              gdn_norm, w_proj_a, w_proj_b, w_out, ln_g, ln_b)
    dec_batch, n_pages = page_table.shape
    yp, ys = x_prompt, x_sample
    kp, vp, sp, bp = [], [], [], []
    ksm, vsm, ssm, bsm = [], [], [], []
    for l in range(DEPTH):
        conv0 = jnp.zeros((yp.shape[0], CONV_W - 1, B_CONV_CH), yp.dtype)
        s0 = jnp.zeros((yp.shape[0], B_HEADS, B_DK, B_DV), state_gdn.dtype)
        yp, k_new, v_new, s_new, buf_new = decoder_layer(yp, c_prompt, l, None, None, conv0, s0, *params)
        kp.append(k_new)
        vp.append(v_new)
        sp.append(s_new)
        bp.append(buf_new)
        k_past = cache_k[l, page_table].reshape(dec_batch, n_pages * PAGE_SIZE, A_HEADS, 2, A_QK_DIM)
        v_past = cache_v[l, page_table].reshape(dec_batch, n_pages * PAGE_SIZE, A_HEADS, A_V_DIM)
        ys, k_new, v_new, s_new, buf_new = decoder_layer(ys, c_sample, l, k_past, v_past, state_conv[l],
                                                         state_gdn[l], *params)
        ksm.append(k_new)
        vsm.append(v_new)
        ssm.append(s_new)
        bsm.append(buf_new)
    return (yp, ys, jnp.stack(kp), jnp.stack(vp), jnp.stack(sp), jnp.stack(bp),
            jnp.stack(ksm), jnp.stack(vsm), jnp.stack(ssm), jnp.stack(bsm))
```

```python
import functools
import math

import jax
import jax.numpy as jnp
from jax import lax
from jax.experimental import pallas as pl
from jax.experimental.pallas import tpu as pltpu

F32 = jnp.float32
BF16 = jnp.bfloat16

D_MODEL = 1024
HEADS = 8
HEAD_W = 128
QK_DIM = 64
CONV_W = 4
LN_EPS = 1e-5
RMS_EPS = 1e-6

OFF_AQ, OFF_AK, OFF_AV, OFF_AZ = 0, 1024, 2048, 3072
OFF_BQ, OFF_BK, OFF_BV, OFF_BZ = 4096, 5120, 6144, 7168
OFF_GA, OFF_GB = 8192, 9216
N_MAIN = 10240
BA_W = 128

VMEM_LIMIT = 56 * 1024 * 1024


def _sigmoid(x):
    return 1.0 / (1.0 + jnp.exp(-x))


def _silu(x):
    return x * _sigmoid(x)


def _softplus(x):
    return jnp.maximum(x, 0.0) + jnp.log1p(jnp.exp(-jnp.abs(x)))


def _mm(a, b):
    return jnp.dot(a.astype(BF16), b.astype(BF16), preferred_element_type=F32)


def _mm_nt(a, b):
    return lax.dot_general(a.astype(BF16), b.astype(BF16), (((1,), (1,)), ((), ())), preferred_element_type=F32)


def _mm_tn(a, b):
    return lax.dot_general(a.astype(BF16), b.astype(BF16), (((0,), (0,)), ((), ())), preferred_element_type=F32)


def _params(*sem):
    return pltpu.CompilerParams(dimension_semantics=sem, vmem_limit_bytes=VMEM_LIMIT)


def _ada_kernel(c_ref, w_ref, b_ref, o_ref):
    c = c_ref[...]
    o_ref[0] = _mm(_silu(c), w_ref[0]) + b_ref[0]


def _ada(c_all, w_ada, b_ada):
    depth, d, n = w_ada.shape
    rows = c_all.shape[0]
    tn = 1024
    return pl.pallas_call(
        _ada_kernel,
        grid=(depth, n // tn),
        in_specs=[
            pl.BlockSpec((rows, d), lambda l, j: (0, 0)),
            pl.BlockSpec((1, d, tn), lambda l, j: (l, 0, j)),
            pl.BlockSpec((1, 1, tn), lambda l, j: (l, 0, j)),
        ],
        out_specs=pl.BlockSpec((1, rows, tn), lambda l, j: (l, 0, j)),
        out_shape=jax.ShapeDtypeStruct((depth, rows, n), F32),
        compiler_params=_params("parallel", "parallel"),
        name="ada",
    )(c_all, w_ada, b_ada.reshape(depth, 1, n))


def _inproj_kernel(x_ref, sc_ref, sh_ref, w_ref, wba_ref, o32_ref, o16_ref, oba_ref, h_ref):
    @pl.when(pl.program_id(1) == 0)
    def _():
        h = (x_ref[...] * (1.0 + sc_ref[0]) + sh_ref[0]).astype(BF16)
        h_ref[...] = h
        oba_ref[...] = jnp.dot(h, wba_ref[...], preferred_element_type=F32)

    acc = jnp.dot(h_ref[...], w_ref[...], preferred_element_type=F32)
    o32_ref[...] = acc
    o16_ref[...] = acc.astype(BF16)


def _inproj(x2d, scale, shift, w_main, w_ba, tm):
    r, d = x2d.shape
    g, sr, _ = scale.shape
    tn = 1024
    rows_per_group = r // g
    mod_spec = pl.BlockSpec((1, sr, d), lambda i, j: ((i * tm) // rows_per_group, 0, 0))
    return pl.pallas_call(
        _inproj_kernel,
        grid=(r // tm, N_MAIN // tn),
        in_specs=[
            pl.BlockSpec((tm, d), lambda i, j: (i, 0)),
            mod_spec,
            mod_spec,
            pl.BlockSpec((d, tn), lambda i, j: (0, j)),
            pl.BlockSpec((d, BA_W), lambda i, j: (0, 0)),
        ],
        out_specs=[
            pl.BlockSpec((tm, tn), lambda i, j: (i, j)),
            pl.BlockSpec((tm, tn), lambda i, j: (i, j)),
            pl.BlockSpec((tm, BA_W), lambda i, j: (i, 0)),
        ],
        out_shape=[
            jax.ShapeDtypeStruct((r, N_MAIN), F32),
            jax.ShapeDtypeStruct((r, N_MAIN), BF16),
            jax.ShapeDtypeStruct((r, BA_W), F32),
        ],
        scratch_shapes=[pltpu.VMEM((tm, d), BF16)],
        compiler_params=_params("parallel", "arbitrary"),
        name="inproj",
    )(x2d, scale, shift, w_main, w_ba)


def _lam_from_rows(lam_ref, lam_init):
    a = jnp.sum(lam_ref[0:1, :] * lam_ref[1:2, :], axis=1, keepdims=True)
    b = jnp.sum(lam_ref[2:3, :] * lam_ref[3:4, :], axis=1, keepdims=True)
    return jnp.exp(a) - jnp.exp(b) + lam_init


def _subln_gate(o, sub_row, z, lam_init):
    o = o * lax.rsqrt(jnp.mean(o * o, axis=1, keepdims=True) + RMS_EPS) * sub_row * (1.0 - lam_init)
    return o * _silu(z.astype(F32))


def _attn_prompt_kernel(slope_ref, lam_ref, sub_ref, q_ref, k_ref, v_ref, z_ref, o_ref, m_ref, l_ref, acc_ref,
                        *, tq, lam_init):
    h = pl.program_id(1)
    i = pl.program_id(2)
    slope = slope_ref[h]
    lane = lax.broadcasted_iota(jnp.int32, (1, HEAD_W), 1)
    qs = q_ref[...] * (QK_DIM ** -0.5)
    zero = jnp.zeros_like(qs)
    qm = (jnp.where(lane < QK_DIM, qs, zero), jnp.where(lane >= QK_DIM, qs, zero))
    m_ref[...] = jnp.full(m_ref.shape, -jnp.inf, F32)
    l_ref[...] = jnp.zeros(l_ref.shape, F32)
    acc_ref[...] = jnp.zeros(acc_ref.shape, F32)
    col = lax.broadcasted_iota(jnp.int32, (1, tq), 1)

    def step(j, masked):
        start = pl.multiple_of(j * tq, tq)
        k = k_ref[pl.ds(start, tq), :]
        v = v_ref[pl.ds(start, tq), :]
        bias = slope * ((j - i) * tq + col).astype(F32)
        for mi in range(2):
            s = lax.dot_general(qm[mi], k, (((1,), (1,)), ((), ())), preferred_element_type=F32) + bias
            if masked:
                r = lax.broadcasted_iota(jnp.int32, (tq, tq), 0)
                c = lax.broadcasted_iota(jnp.int32, (tq, tq), 1)
                s = jnp.where(c <= r, s, -jnp.inf)
            m_prev = m_ref[mi]
            m_new = jnp.maximum(m_prev, jnp.max(s, axis=1, keepdims=True))
            alpha = jnp.exp(m_prev - m_new)
            p = jnp.exp(s - jnp.tile(m_new, (1, tq // HEAD_W)))
            l_ref[mi] = alpha * l_ref[mi] + jnp.sum(p, axis=1, keepdims=True)
            acc_ref[mi] = alpha * acc_ref[mi] + jnp.dot(p.astype(BF16), v, preferred_element_type=F32)
            m_ref[mi] = m_new

    def body(j, carry):
        step(j, False)
        return carry

    lax.fori_loop(0, i, body, 0)
    step(i, True)

    lam = _lam_from_rows(lam_ref, lam_init)
    o = acc_ref[0] / l_ref[0] - lam * (acc_ref[1] / l_ref[1])
    o_ref[...] = _subln_gate(o, sub_ref[...], z_ref[...], lam_init).astype(BF16)


def _attn_prompt(p16, slopes, lam_rows, sub_row, bsz, t, lam_init, tq):
    nq = t // tq
    kern = functools.partial(_attn_prompt_kernel, tq=tq, lam_init=lam_init)
    return pl.pallas_call(
        kern,
        grid=(bsz, HEADS, nq),
        in_specs=[
            pl.BlockSpec(memory_space=pltpu.SMEM),
            pl.BlockSpec((8, HEAD_W), lambda b, h, i: (0, 0)),
            pl.BlockSpec((1, HEAD_W), lambda b, h, i: (0, 0)),
            pl.BlockSpec((tq, HEAD_W), lambda b, h, i: (b * nq + i, OFF_AQ // HEAD_W + h)),
            pl.BlockSpec((t, HEAD_W), lambda b, h, i: (b, OFF_AK // HEAD_W + h)),
            pl.BlockSpec((t, HEAD_W), lambda b, h, i: (b, OFF_AV // HEAD_W + h)),
            pl.BlockSpec((tq, HEAD_W), lambda b, h, i: (b * nq + i, OFF_AZ // HEAD_W + h)),
        ],
        out_specs=pl.BlockSpec((tq, HEAD_W), lambda b, h, i: (b * nq + i, h)),
        out_shape=jax.ShapeDtypeStruct((bsz * t, HEADS * HEAD_W), BF16),
        scratch_shapes=[pltpu.VMEM((2, tq, HEAD_W), F32)] * 3,
        compiler_params=_params("parallel", "parallel", "arbitrary"),
        name="attn_prompt",
    )(slopes, lam_rows, sub_row, p16, p16, p16, p16)


def _attn_sample_kernel(pt_ref, sloper_ref, lam_ref, sub_ref, q_ref, kn_ref, vn_ref, z_ref, kc_ref, vc_ref, o_ref,
                        qbd_ref, m_ref, l_ref, acc_ref, *, n_pages, past, page, dt, lam_init):
    del pt_ref
    pg = pl.program_id(1)
    rows = HEADS * 2 * dt
    grp = 2 * dt

    @pl.when(pg == 0)
    def _():
        q = q_ref[0].astype(F32) * (QK_DIM ** -0.5)
        qt = jnp.tile(q, (HEADS * 2, 1))
        r = lax.broadcasted_iota(jnp.int32, (rows, D_MODEL), 0)
        c = lax.broadcasted_iota(jnp.int32, (rows, D_MODEL), 1)
        qbd_ref[...] = jnp.where(r // dt == c // QK_DIM, qt, 0.0).astype(BF16)
        m_ref[...] = jnp.full(m_ref.shape, -jnp.inf, F32)
        l_ref[...] = jnp.zeros(l_ref.shape, F32)
        acc_ref[...] = jnp.zeros(acc_ref.shape, F32)

    lane = lax.broadcasted_iota(jnp.int32, (1, page), 1)
    s = jnp.dot(qbd_ref[...], kc_ref[...].astype(BF16), preferred_element_type=F32)
    s = s + sloper_ref[...] * (pg * page - past + lane).astype(F32)
    m_prev = m_ref[...]
    m_new = jnp.maximum(m_prev, jnp.max(s, axis=1, keepdims=True))
    alpha = jnp.exp(m_prev - m_new)
    p = jnp.exp(s - m_new)
    l_ref[...] = alpha * l_ref[...] + jnp.sum(p, axis=1, keepdims=True)
    m_ref[...] = m_new
    pb = p.astype(BF16)
    for h in range(HEADS):
        vh = vc_ref[pl.ds(h, page, stride=HEADS), :].astype(BF16)
        sl = slice(h * grp, (h + 1) * grp)
        acc_ref[sl, :] = alpha[sl] * acc_ref[sl, :] + jnp.dot(pb[sl], vh, preferred_element_type=F32)

    @pl.when(pg == n_pages - 1)
    def _():
        qf = qbd_ref[...].astype(F32)
        kn = kn_ref[0].astype(F32)
        vn = vn_ref[0].astype(F32)
        rq = lax.broadcasted_iota(jnp.int32, (rows, 1), 0) % dt
        slope_col = sloper_ref[:, 0:1]
        s_new = []
        for j in range(dt):
            sj = jnp.sum(qf * kn[j:j + 1, :], axis=1, keepdims=True) + slope_col * float(j)
            s_new.append(jnp.where(rq >= j, sj, -jnp.inf))
        m_prev2 = m_ref[...]
        m_cur = s_new[0]
        for j in range(1, dt):
            m_cur = jnp.maximum(m_cur, s_new[j])
        m_fin = jnp.maximum(m_prev2, m_cur)
        a2 = jnp.exp(m_prev2 - m_fin)
        l_fin = a2 * l_ref[...]
        acc = a2 * acc_ref[...]
        for j in range(dt):
            pj = jnp.exp(s_new[j] - m_fin)
            l_fin = l_fin + pj
            vexp = jnp.concatenate(
                [jnp.broadcast_to(vn[j:j + 1, h * HEAD_W:(h + 1) * HEAD_W], (grp, HEAD_W)) for h in range(HEADS)], axis=0)
            acc = acc + pj * vexp
        on = acc / l_fin
        lam = _lam_from_rows(lam_ref, lam_init)
        for h in range(HEADS):
            blk = on[h * grp:(h + 1) * grp]
            o = blk[0:dt] - lam * blk[dt:grp]
            cs = slice(h * HEAD_W, (h + 1) * HEAD_W)
            o_ref[0, :, cs] = _subln_gate(o, sub_ref[...], z_ref[0][:, cs], lam_init).astype(BF16)


def _attn_sample(p16s, page_table, slope_rows, lam_rows, sub_row, cache_kt, cache_vr, layer, lam_init):
    db, dt, _ = p16s.shape
    n_pages = page_table.shape[1]
    page = cache_kt.shape[-1]
    rows = HEADS * 2 * dt
    kern = functools.partial(_attn_sample_kernel, n_pages=n_pages, past=n_pages * page, page=page, dt=dt,
                             lam_init=lam_init)
    blk = lambda c: pl.BlockSpec((1, dt, D_MODEL), lambda b, g, pt: (b, 0, c))
    cache_spec = pl.BlockSpec((None, None, HEADS * HEAD_W, page),
                              lambda b, g, pt: (layer, pt[b * n_pages + g], 0, 0))
    grid_spec = pltpu.PrefetchScalarGridSpec(
        num_scalar_prefetch=1,
        grid=(db, n_pages),
        in_specs=[
            pl.BlockSpec((rows, HEAD_W), lambda b, g, pt: (0, 0)),
            pl.BlockSpec((8, HEAD_W), lambda b, g, pt: (0, 0)),
            pl.BlockSpec((1, HEAD_W), lambda b, g, pt: (0, 0)),
            blk(OFF_AQ // D_MODEL), blk(OFF_AK // D_MODEL), blk(OFF_AV // D_MODEL), blk(OFF_AZ // D_MODEL),
            cache_spec, cache_spec,
        ],
        out_specs=pl.BlockSpec((1, dt, D_MODEL), lambda b, g, pt: (b, 0, 0)),
        scratch_shapes=[
            pltpu.VMEM((rows, D_MODEL), BF16),
            pltpu.VMEM((rows, HEAD_W), F32),
            pltpu.VMEM((rows, HEAD_W), F32),
            pltpu.VMEM((rows, HEAD_W), F32),
        ],
    )
    return pl.pallas_call(
        kern,
        grid_spec=grid_spec,
        out_shape=jax.ShapeDtypeStruct((db, dt, D_MODEL), BF16),
        compiler_params=_params("parallel", "arbitrary"),
        name="attn_sample",
    )(page_table.reshape(-1), slope_rows, lam_rows, sub_row, p16s, p16s, p16s, p16s, cache_kt, cache_vr)


def _gates(ba, alog_row, dtb_row):
    beta = _sigmoid(ba)
    g = -jnp.exp(alog_row) * _softplus(ba + dtb_row)
    return beta, g


def _lane_col(x, idx):
    lane = lax.broadcasted_iota(jnp.int32, x.shape, 1)
    return jnp.sum(jnp.where(lane == idx, x, 0.0), axis=1, keepdims=True)


def _l2n(x):
    return x * lax.rsqrt(jnp.sum(x * x, axis=1, keepdims=True) + RMS_EPS)


def _conv_silu(xb_ref, part, x_rows, cw_ref, n_rows):
    xb_ref[part, 8:8 + x_rows.shape[0], :] = x_rows
    acc = xb_ref[part, 5:5 + n_rows, :] * cw_ref[0:1, :]
    for j in range(1, CONV_W):
        acc = acc + xb_ref[part, 5 + j:5 + j + n_rows, :] * cw_ref[j:j + 1, :]
    return _silu(acc)


def _gdn_prompt_kernel(xq_ref, xk_ref, xv_ref, z_ref, ba_ref, cwq_ref, cwk_ref, cwv_ref, csq_ref, csk_ref, csv_ref,
                       s0_ref, alog_ref, dtb_ref, norm_ref, o_ref, sout_ref, st_ref, xb_ref, *, chunk, hb):
    hg = pl.program_id(1)
    t = pl.program_id(2)
    nt = pl.num_programs(2)
    c = chunk
    w = hb * HEAD_W

    @pl.when(t == 0)
    def _():
        st_ref[...] = s0_ref[0]
        for part, cs_ref in enumerate((csq_ref, csk_ref, csv_ref)):
            xb_ref[part, 0:8, :] = jnp.zeros((8, w), F32)
            xb_ref[part, 5:8, :] = cs_ref[0]

    acts = []
    for part, (x_ref, cw_ref) in enumerate(((xq_ref, cwq_ref), (xk_ref, cwk_ref), (xv_ref, cwv_ref))):
        acts.append(_conv_silu(xb_ref, part, x_ref[0], cw_ref, c))
        xb_ref[part, 0:8, :] = xb_ref[part, c:c + 8, :]

    beta_blk, g_blk = _gates(ba_ref[0], alog_ref[...], dtb_ref[...])
    row = lax.broadcasted_iota(jnp.int32, (c, BA_W), 0)
    gc_blk = g_blk
    sh = 1
    while sh < c:
        gc_blk = gc_blk + jnp.where(row >= sh, pltpu.roll(gc_blk, sh, 0), 0.0)
        sh *= 2

    ri = lax.broadcasted_iota(jnp.int32, (c, c), 0)
    ci = lax.broadcasted_iota(jnp.int32, (c, c), 1)
    causal = ci <= ri
    strict = ci < ri
    eye = ci == ri
    ident = jnp.where(eye, 1.0, 0.0)
    n_lvl = int(math.log2(c))
    same = [(ri >> k) == (ci >> k) for k in range(n_lvl + 1)]
    lvl_mask = [jnp.where(same[k], 0.0, jnp.where(same[k + 1], 1.0, 0.0)) * jnp.where(strict, 1.0, 0.0)
                for k in range(n_lvl)]

    for hh in range(hb):
        hidx = hg * hb + hh
        bt = _lane_col(beta_blk, hidx)
        gcol = _lane_col(gc_blk, HEADS + hidx)
        sl = slice(hh * HEAD_W, (hh + 1) * HEAD_W)
        qn = _l2n(acts[0][:, sl]) * (HEAD_W ** -0.5)
        kn = _l2n(acts[1][:, sl])
        v = acts[2][:, sl]
        gcb = jnp.broadcast_to(gcol, (c, c))
        grow = jnp.sum(jnp.where(eye, gcb, 0.0), axis=0, keepdims=True)
        dec = jnp.where(causal, jnp.exp(jnp.minimum(gcb - grow, 0.0)), 0.0)
        kb = kn * bt
        lm = _mm_nt(kb, kn) * dec
        minv = ident - lvl_mask[0] * lm
        for k in range(1, n_lvl):
            minv = minv - _mm(_mm(minv, lvl_mask[k] * lm), minv)
        eg = jnp.exp(gcol)
        sol = _mm(minv, jnp.concatenate([v * bt, kb * eg], axis=1))
        u = sol[:, :HEAD_W]
        wm = sol[:, HEAD_W:]
        qk = _mm_nt(qn, kn) * dec
        glast = gcol[c - 1:c, :]
        kd = kn * jnp.exp(glast - gcol)
        s_h = st_ref[hh]
        vnew = u - _mm(wm, s_h)
        o = _mm(qn * eg, s_h) + _mm(qk, vnew)
        st_ref[hh] = s_h * jnp.exp(glast) + _mm_tn(kd, vnew)
        on = o * lax.rsqrt(jnp.mean(o * o, axis=1, keepdims=True) + RMS_EPS) * norm_ref[...]
        o_ref[0, :, sl] = (on * _silu(z_ref[0][:, sl].astype(F32))).astype(BF16)

    @pl.when(t == nt - 1)
    def _():
        sout_ref[0] = st_ref[...]


def _gdn_prompt(p32, p16, pba, conv_w_l, conv_state, s0, alog_row, dtb_row, norm_row, chunk, hb):
    bsz, t, _ = p32.shape
    w = hb * HEAD_W
    xs = lambda off: pl.BlockSpec((1, chunk, w), lambda b, g, i: (b, i, off // w + g))
    cws = lambda part: pl.BlockSpec((CONV_W, w), lambda b, g, i: (0, part * (D_MODEL // w) + g))
    css = lambda part: pl.BlockSpec((1, CONV_W - 1, w), lambda b, g, i: (b, 0, part * (D_MODEL // w) + g))
    row = pl.BlockSpec((1, HEAD_W), lambda b, g, i: (0, 0))
    kern = functools.partial(_gdn_prompt_kernel, chunk=chunk, hb=hb)
    return pl.pallas_call(
        kern,
        grid=(bsz, HEADS // hb, t // chunk),
        in_specs=[
            xs(OFF_BQ), xs(OFF_BK), xs(OFF_BV), xs(OFF_BZ),
            pl.BlockSpec((1, chunk, BA_W), lambda b, g, i: (b, i, 0)),
            cws(0), cws(1), cws(2), css(0), css(1), css(2),
            pl.BlockSpec((1, hb, HEAD_W, HEAD_W), lambda b, g, i: (b, g, 0, 0)),
            row, row, row,
        ],
        out_specs=[
            pl.BlockSpec((1, chunk, w), lambda b, g, i: (b, i, g)),
            pl.BlockSpec((1, hb, HEAD_W, HEAD_W), lambda b, g, i: (b, g, 0, 0)),
        ],
        out_shape=[
            jax.ShapeDtypeStruct((bsz, t, HEADS * HEAD_W), BF16),
            jax.ShapeDtypeStruct((bsz, HEADS, HEAD_W, HEAD_W), F32),
        ],
        scratch_shapes=[pltpu.VMEM((hb, HEAD_W, HEAD_W), F32), pltpu.VMEM((3, chunk + 8, w), F32)],
        compiler_params=_params("parallel", "parallel", "arbitrary"),
        name="gdn_prompt",
    )(p32, p32, p32, p16, pba, conv_w_l, conv_w_l, conv_w_l, conv_state, conv_state, conv_state, s0,
      alog_row, dtb_row, norm_row)


def _gdn_sample_kernel(xq_ref, xk_ref, xv_ref, z_ref, ba_ref, cw_ref, cs_ref, s0_ref, alog_ref, dtb_ref, norm_ref,
                       o_ref, sout_ref, xb_ref, *, dt):
    w = HEADS * HEAD_W
    acts = []
    for part, x_ref in enumerate((xq_ref, xk_ref, xv_ref)):
        xb_ref[part, 0:8, :] = jnp.zeros((8, w), F32)
        xb_ref[part, 8:16, :] = jnp.zeros((8, w), F32)
        xb_ref[part, 5:8, :] = cs_ref[0][:, part * w:(part + 1) * w]
        acts.append(_conv_silu(xb_ref, part, x_ref[0], cw_ref.at[:, part * w:(part + 1) * w], 8))

    beta_blk, g_blk = _gates(ba_ref[0], alog_ref[...], dtb_ref[...])
    rowi = lax.broadcasted_iota(jnp.int32, (8, HEAD_W), 0)

    for h in range(HEADS):
        sl = slice(h * HEAD_W, (h + 1) * HEAD_W)
        bt = _lane_col(beta_blk, h)
        g = _lane_col(g_blk, HEADS + h)
        qn = _l2n(acts[0][:, sl]) * (HEAD_W ** -0.5)
        kn = _l2n(acts[1][:, sl])
        v = acts[2][:, sl]
        s0 = s0_ref[0, h]
        kq = jnp.concatenate([kn, qn], axis=0)
        ks0 = _mm(kq, s0)
        dots = [jnp.sum(kq * kn[j:j + 1, :], axis=1, keepdims=True) for j in range(dt)]
        gc = [g[0:1, :]]
        for tt in range(1, dt):
            gc.append(gc[-1] + g[tt:tt + 1, :])
        d = []
        outs = []
        for tt in range(dt):
            if tt == 0:
                r = ks0[0:1, :]
            else:
                r = jnp.exp(gc[tt - 1]) * ks0[tt:tt + 1, :]
                for j in range(tt):
                    r = r + (jnp.exp(gc[tt - 1] - gc[j]) * dots[j][tt:tt + 1, :]) * d[j]
            d_t = bt[tt:tt + 1, :] * (v[tt:tt + 1, :] - jnp.exp(g[tt:tt + 1, :]) * r)
            d.append(d_t)
            o_t = jnp.exp(gc[tt]) * ks0[8 + tt:9 + tt, :]
            for j in range(tt + 1):
                o_t = o_t + (jnp.exp(gc[tt] - gc[j]) * dots[j][8 + tt:9 + tt, :]) * d[j]
            outs.append(o_t)
        dm = jnp.zeros((8, HEAD_W), F32)
        kd = jnp.zeros((8, HEAD_W), F32)
        om = jnp.zeros((8, HEAD_W), F32)
        for tt in range(dt):
            dm = jnp.where(rowi == tt, d[tt], dm)
            kd = jnp.where(rowi == tt, jnp.exp(gc[dt - 1] - gc[tt]) * kn[tt:tt + 1, :], kd)
            om = jnp.where(rowi == tt, outs[tt], om)
        sout_ref[0, h] = s0 * jnp.exp(gc[dt - 1]) + _mm_tn(kd, dm)
        on = om * lax.rsqrt(jnp.mean(om * om, axis=1, keepdims=True) + RMS_EPS) * norm_ref[...]
        zz = z_ref[0][:, sl].astype(F32)
        o_ref[0, :, sl] = (on[0:dt] * _silu(zz)).astype(BF16)


def _gdn_sample(p32s, p16s, pbas, conv_w_l, conv_state, s0, alog_row, dtb_row, norm_row):
    db, dt, _ = p32s.shape
    w = HEADS * HEAD_W
    xs = lambda off: pl.BlockSpec((1, dt, w), lambda b: (b, 0, off // w))
    row = pl.BlockSpec((1, HEAD_W), lambda b: (0, 0))
    st = pl.BlockSpec((1, HEADS, HEAD_W, HEAD_W), lambda b: (b, 0, 0, 0))
    kern = functools.partial(_gdn_sample_kernel, dt=dt)
    return pl.pallas_call(
        kern,
        grid=(db,),
        in_specs=[
            xs(OFF_BQ), xs(OFF_BK), xs(OFF_BV), xs(OFF_BZ),
            pl.BlockSpec((1, dt, BA_W), lambda b: (b, 0, 0)),
            pl.BlockSpec((CONV_W, 3 * w), lambda b: (0, 0)),
            pl.BlockSpec((1, CONV_W - 1, 3 * w), lambda b: (b, 0, 0)),
            st, row, row, row,
        ],
        out_specs=[pl.BlockSpec((1, dt, w), lambda b: (b, 0, 0)), st],
        out_shape=[
            jax.ShapeDtypeStruct((db, dt, w), BF16),
            jax.ShapeDtypeStruct((db, HEADS, HEAD_W, HEAD_W), F32),
        ],
        scratch_shapes=[pltpu.VMEM((3, 16, w), F32)],
        compiler_params=_params("parallel"),
        name="gdn_sample",
    )(p32s, p32s, p32s, p16s, pbas, conv_w_l, conv_state, s0, alog_row, dtb_row, norm_row)


def _merge_kernel(oa_ref, ob_ref, ga_ref, gb_ref, x_ref, gate_ref, wpa_ref, wpb_ref, wo_ref, lng_ref, lnb_ref, o_ref,
                  *, alpha):
    ya = jnp.dot(oa_ref[...], wpa_ref[...], preferred_element_type=F32)
    yb = jnp.dot(ob_ref[...], wpb_ref[...], preferred_element_type=F32)
    y = _sigmoid(ga_ref[...].astype(F32)) * ya + _sigmoid(gb_ref[...].astype(F32)) * yb
    out = jnp.dot(y.astype(BF16), wo_ref[...], preferred_element_type=F32)
    zz = alpha * x_ref[...] + gate_ref[0] * out
    zc = zz - jnp.mean(zz, axis=1, keepdims=True)
    var = jnp.mean(zc * zc, axis=1, keepdims=True)
    o_ref[...] = zc * lax.rsqrt(var + LN_EPS) * lng_ref[...] + lnb_ref[...]


def _merge(oa, ob, p16, x2d, gate, wpa, wpb, wo, lng, lnb, alpha, tm):
    r, d = x2d.shape
    g, sr, _ = gate.shape
    rows_per_group = r // g
    act = pl.BlockSpec((tm, d), lambda i: (i, 0))
    wsp = pl.BlockSpec((d, d), lambda i: (0, 0))
    vec = pl.BlockSpec((1, d), lambda i: (0, 0))
    kern = functools.partial(_merge_kernel, alpha=alpha)
    return pl.pallas_call(
        kern,
        grid=(r // tm,),
        in_specs=[
            act, act,
            pl.BlockSpec((tm, d), lambda i: (i, OFF_GA // d)),
            pl.BlockSpec((tm, d), lambda i: (i, OFF_GB // d)),
            act,
            pl.BlockSpec((1, sr, d), lambda i: ((i * tm) // rows_per_group, 0, 0)),
            wsp, wsp, wsp, vec, vec,
        ],
        out_specs=act,
        out_shape=jax.ShapeDtypeStruct((r, d), F32),
        compiler_params=_params("parallel"),
        name="merge",
    )(oa, ob, p16, p16, x2d, gate, wpa, wpb, wo, lng, lnb)


def _reorder_w_in(w):
    n_head_cols = OFF_GA
    main = jnp.concatenate([w[:, :n_head_cols], w[:, n_head_cols + 2 * HEADS:]], axis=1)
    ba = jnp.pad(w[:, n_head_cols:n_head_cols + 2 * HEADS], ((0, 0), (0, BA_W - 2 * HEADS)))
    return main.astype(BF16), ba.astype(BF16)


def _pad_lanes(v, offset=0):
    return jnp.pad(v.astype(F32), (offset, HEAD_W - offset - v.shape[0])).reshape(1, HEAD_W)


def _kv_rows(p32, lead):
    k = p32[:, OFF_AK:OFF_AK + D_MODEL].reshape(lead + (HEADS, 2, QK_DIM))
    v = p32[:, OFF_AV:OFF_AV + D_MODEL].reshape(lead + (HEADS, HEAD_W))
    return k, v


def _layer(x, mod, layer, prm, sample_state, depth):
    bsz, t, d = x.shape
    r = bsz * t
    assert t >= CONV_W - 1
    lam_init = 0.8 - 0.6 * math.exp(-0.3 * layer)
    alpha = (2 * depth) ** 0.25
    shift, scale, gate = jnp.split(mod, 3, axis=-1)
    x2d = x.reshape(r, d)
    if sample_state is None:
        tm = min(1024, t)
        grp = lambda m: m[:, None, :]
    else:
        tm = r
        grp = lambda m: jnp.repeat(m, t, axis=0)[None]
    p32, p16, pba = _inproj(x2d, grp(scale), grp(shift), prm["w_main"][layer], prm["w_ba"][layer], tm)
    p32_3, p16_3, pba_3 = (a.reshape(bsz, t, -1) for a in (p32, p16, pba))
    sub_row = prm["subln_a"][layer].reshape(1, HEAD_W)
    gdn_rows = (prm["alog_row"][layer], prm["dtb_row"][layer], prm["gdn_norm"][layer].reshape(1, HEAD_W))
    conv_w_l = prm["conv_w"][layer]
    if sample_state is None:
        o_a = _attn_prompt(p16, prm["slopes"], prm["lam_rows"][layer], sub_row, bsz, t, lam_init, tq=min(512, t))
        conv0 = jnp.zeros((bsz, CONV_W - 1, 3 * d), F32)
        s0 = jnp.zeros((bsz, HEADS, HEAD_W, HEAD_W), F32)
        o_b, s_new = _gdn_prompt(p32_3, p16_3, pba_3, conv_w_l, conv0, s0, *gdn_rows, chunk=min(128, t), hb=4)
        o_b = o_b.reshape(r, d)
    else:
        page_table, cache_kt, cache_vr, conv_state, s0 = sample_state
        slope_rows = jnp.broadcast_to(jnp.repeat(prm["slopes"], 2 * t)[:, None], (HEADS * 2 * t, HEAD_W))
        o_a = _attn_sample(p16_3, page_table, slope_rows, prm["lam_rows"][layer], sub_row, cache_kt, cache_vr,
                           layer, lam_init).reshape(r, d)
        o_b, s_new = _gdn_sample(p32_3, p16_3, pba_3, conv_w_l, conv_state, s0, *gdn_rows)
        o_b = o_b.reshape(r, d)
    x_new = _merge(o_a, o_b, p16, x2d, grp(gate), prm["w_proj_a"][layer], prm["w_proj_b"][layer],
                   prm["w_out"][layer], prm["ln_g"][layer].reshape(1, d), prm["ln_b"][layer].reshape(1, d),
                   alpha, tm=min(512, r))
    k_new, v_new = _kv_rows(p32, (bsz, t))
    buf_new = p32_3[:, t - (CONV_W - 1):, OFF_BQ:OFF_BQ + 3 * d]
    return x_new.reshape(bsz, t, d), k_new, v_new, s_new, buf_new


def kernel(x_prompt, x_sample, c_prompt, c_sample, cache_k, cache_v, state_gdn, state_conv, page_table, w_ada, b_ada, w_in, lambda_q1, lambda_k1, lambda_q2, lambda_k2, subln_a, conv_w, a_log, dt_bias, gdn_norm, w_proj_a, w_proj_b, w_out, ln_g, ln_b):
    depth = w_in.shape[0]
    bsz = x_prompt.shape[0]
    db = x_sample.shape[0]
    n_pool, page = cache_k.shape[1], cache_k.shape[2]

    rows = -(-(bsz + db) // 8) * 8
    c_all = jnp.pad(jnp.concatenate([c_prompt, c_sample], axis=0), ((0, rows - bsz - db), (0, 0)))
    mod = _ada(c_all, w_ada, b_ada)

    w_main, w_ba = zip(*[_reorder_w_in(w_in[l]) for l in range(depth)])
    lam_rows = [jnp.pad(jnp.stack([lambda_q1[l], lambda_k1[l], lambda_q2[l], lambda_k2[l]]).astype(F32),
                        ((0, 4), (0, HEAD_W - QK_DIM))) for l in range(depth)]
    prm = dict(
        w_main=w_main, w_ba=w_ba, lam_rows=lam_rows, subln_a=subln_a, conv_w=conv_w, gdn_norm=gdn_norm,
        slopes=2.0 ** (-8.0 * jnp.arange(1, HEADS + 1, dtype=F32) / HEADS),
        alog_row=[_pad_lanes(a_log[l], HEADS) for l in range(depth)],
        dtb_row=[_pad_lanes(dt_bias[l], HEADS) for l in range(depth)],
        w_proj_a=w_proj_a.astype(BF16), w_proj_b=w_proj_b.astype(BF16), w_out=w_out.astype(BF16),
        ln_g=ln_g, ln_b=ln_b,
    )
    cache_kt = jnp.transpose(cache_k, (0, 1, 3, 4, 5, 2)).reshape(depth, n_pool, HEADS * 2 * QK_DIM, page)
    cache_vr = cache_v.reshape(depth, n_pool, page * HEADS, HEAD_W)

    yp, ys = x_prompt, x_sample
    outs_p, outs_s = [], []
    for l in range(depth):
        yp, *rest = _layer(yp, mod[l, :bsz], l, prm, None, depth)
        outs_p.append(rest)
        state = (page_table, cache_kt, cache_vr, state_conv[l], state_gdn[l])
        ys, *rest = _layer(ys, mod[l, bsz:bsz + db], l, prm, state, depth)
        outs_s.append(rest)
    stack = lambda outs, i: jnp.stack([o[i] for o in outs])
    return (yp, ys, stack(outs_p, 0), stack(outs_p, 1), stack(outs_p, 2), stack(outs_p, 3),
            stack(outs_s, 0), stack(outs_s, 1), stack(outs_s, 2), stack(outs_s, 3))
```

```python
import functools
import math

import jax
import jax.numpy as jnp
from jax import lax
from jax.experimental import pallas as pl
from jax.experimental.pallas import tpu as pltpu

F32 = jnp.float32
BF16 = jnp.bfloat16

D_MODEL = 1024
HEADS = 8
HEAD_W = 128
QK_DIM = 64
CONV_W = 4
LN_EPS = 1e-5
RMS_EPS = 1e-6

OFF_AQ, OFF_AK, OFF_AV, OFF_AZ = 0, 1024, 2048, 3072
OFF_BQ, OFF_BK, OFF_BV, OFF_BZ = 4096, 5120, 6144, 7168
OFF_GA, OFF_GB = 8192, 9216
N_MAIN = 10240
BA_W = 128

VMEM_LIMIT = 56 * 1024 * 1024
LOG2E = math.log2(math.e)
Q_PRESCALE = QK_DIM ** -0.5 * LOG2E
INPROJ_TN = 1024
ATTN_TQ = 512
ATTN_KW = 2
GDN_CHUNK = 128
GDN_NC = 2
GDN_HB = 8


def _sigmoid(x):
    return 1.0 / (1.0 + jnp.exp(-x))


def _silu(x):
    return x * _sigmoid(x)


def _softplus(x):
    return jnp.maximum(x, 0.0) + jnp.log1p(jnp.exp(-jnp.abs(x)))


def _mm(a, b):
    return jnp.dot(a.astype(BF16), b.astype(BF16), preferred_element_type=F32)


def _mm_nt(a, b):
    return lax.dot_general(a.astype(BF16), b.astype(BF16), (((1,), (1,)), ((), ())), preferred_element_type=F32)


def _mm_tn(a, b):
    return lax.dot_general(a.astype(BF16), b.astype(BF16), (((0,), (0,)), ((), ())), preferred_element_type=F32)


def _params(*sem):
    return pltpu.CompilerParams(dimension_semantics=sem, vmem_limit_bytes=VMEM_LIMIT)


def _ada_kernel(c_ref, w_ref, b_ref, o_ref):
    c = c_ref[...]
    o_ref[0] = _mm(_silu(c), w_ref[0]) + b_ref[0]


def _ada(c_all, w_ada, b_ada):
    depth, d, n = w_ada.shape
    rows = c_all.shape[0]
    tn = 1024
    return pl.pallas_call(
        _ada_kernel,
        grid=(depth, n // tn),
        in_specs=[
            pl.BlockSpec((rows, d), lambda l, j: (0, 0)),
            pl.BlockSpec((1, d, tn), lambda l, j: (l, 0, j)),
            pl.BlockSpec((1, 1, tn), lambda l, j: (l, 0, j)),
        ],
        out_specs=pl.BlockSpec((1, rows, tn), lambda l, j: (l, 0, j)),
        out_shape=jax.ShapeDtypeStruct((depth, rows, n), F32),
        compiler_params=_params("parallel", "parallel"),
        name="ada",
    )(c_all, w_ada, b_ada.reshape(depth, 1, n))


def _inproj_kernel(x_ref, sc_ref, sh_ref, w_ref, wba_ref, o32_ref, o16_ref, oba_ref, h_ref):
    @pl.when(pl.program_id(1) == 0)
    def _():
        h = (x_ref[...] * (1.0 + sc_ref[0]) + sh_ref[0]).astype(BF16)
        h_ref[...] = h
        oba_ref[...] = jnp.dot(h, wba_ref[...], preferred_element_type=F32)

    acc = jnp.dot(h_ref[...], w_ref[...], preferred_element_type=F32)
    o32_ref[...] = acc
    o16_ref[...] = (acc * jnp.where(pl.program_id(1) == 0, Q_PRESCALE, 1.0)).astype(BF16)


def _inproj(x2d, scale, shift, w_main, w_ba, tm):
    r, d = x2d.shape
    g, sr, _ = scale.shape
    tn = INPROJ_TN
    assert tn == OFF_AK - OFF_AQ
    rows_per_group = r // g
    mod_spec = pl.BlockSpec((1, sr, d), lambda i, j: ((i * tm) // rows_per_group, 0, 0))
    return pl.pallas_call(
        _inproj_kernel,
        grid=(r // tm, N_MAIN // tn),
        in_specs=[
            pl.BlockSpec((tm, d), lambda i, j: (i, 0)),
            mod_spec,
            mod_spec,
            pl.BlockSpec((d, tn), lambda i, j: (0, j)),
            pl.BlockSpec((d, BA_W), lambda i, j: (0, 0)),
        ],
        out_specs=[
            pl.BlockSpec((tm, tn), lambda i, j: (i, j)),
            pl.BlockSpec((tm, tn), lambda i, j: (i, j)),
            pl.BlockSpec((tm, BA_W), lambda i, j: (i, 0)),
        ],
        out_shape=[
            jax.ShapeDtypeStruct((r, N_MAIN), F32),
            jax.ShapeDtypeStruct((r, N_MAIN), BF16),
            jax.ShapeDtypeStruct((r, BA_W), F32),
        ],
        scratch_shapes=[pltpu.VMEM((tm, d), BF16)],
        compiler_params=_params("parallel", "arbitrary"),
        name="inproj",
    )(x2d, scale, shift, w_main, w_ba)


def _lam_from_rows(lam_ref, lam_init):
    a = jnp.sum(lam_ref[0:1, :] * lam_ref[1:2, :], axis=1, keepdims=True)
    b = jnp.sum(lam_ref[2:3, :] * lam_ref[3:4, :], axis=1, keepdims=True)
    return jnp.exp(a) - jnp.exp(b) + lam_init


def _subln_gate(o, sub_row, z, lam_init):
    o = o * lax.rsqrt(jnp.mean(o * o, axis=1, keepdims=True) + RMS_EPS) * sub_row * (1.0 - lam_init)
    return o * _silu(z.astype(F32))


def _attn_prompt_kernel(slope_ref, lam_ref, sub_ref, q_ref, k_ref, v_ref, z_ref, o_ref, m_ref, acc_ref,
                        *, tq, kw, lam_init):
    h = pl.program_id(1)
    i = pl.program_id(2)
    slope2 = slope_ref[h] * LOG2E
    lane = lax.broadcasted_iota(jnp.int32, (1, HEAD_W), 1)
    qs = q_ref[...]
    zero = jnp.zeros_like(qs)
    qm = (jnp.where(lane < QK_DIM, qs, zero), jnp.where(lane >= QK_DIM, qs, zero))
    m_ref[...] = jnp.full(m_ref.shape, -jnp.inf, F32)
    acc_ref[...] = jnp.zeros(acc_ref.shape, F32)
    def step(start, width, masked):
        k = k_ref[pl.ds(start, width), :]
        v1 = jnp.concatenate([v_ref[pl.ds(start, width), :], jnp.ones((width, HEAD_W), BF16)], axis=1)
        col = lax.broadcasted_iota(jnp.int32, (1, width), 1)
        bias = slope2 * (start - i * tq + col).astype(F32)
        ss = [lax.dot_general(qm[mi], k, (((1,), (1,)), ((), ())), preferred_element_type=F32) for mi in range(2)]
        ps, alphas = [], []
        for mi in range(2):
            s = ss[mi] + bias
            if masked:
                r = lax.broadcasted_iota(jnp.int32, (tq, width), 0)
                c = lax.broadcasted_iota(jnp.int32, (tq, width), 1)
                s = jnp.where(c <= r, s, -jnp.inf)
            m_prev = m_ref[mi]
            m_new = jnp.maximum(m_prev, jnp.max(s, axis=1, keepdims=True))
            alphas.append(jnp.exp2(m_prev - m_new))
            ps.append(jnp.exp2(s - jnp.tile(m_new, (1, width // HEAD_W))).astype(BF16))
            m_ref[mi] = m_new
        for mi in range(2):
            acc_ref[mi] = jnp.tile(alphas[mi], (1, 2)) * acc_ref[mi] + jnp.dot(ps[mi], v1,
                                                                                preferred_element_type=F32)

    def body(jj, carry):
        step(pl.multiple_of(jj * (kw * tq), kw * tq), kw * tq, False)
        return carry

    lax.fori_loop(0, i // kw, body, 0)
    for rem in range(1, kw):
        @pl.when(i % kw >= rem)
        def _():
            step(pl.multiple_of((i - (i % kw) + rem - 1) * tq, tq), tq, False)
    step(pl.multiple_of(i * tq, tq), tq, True)

    lam = _lam_from_rows(lam_ref, lam_init)
    a0, a1 = acc_ref[0], acc_ref[1]
    o = a0[:, :HEAD_W] / a0[:, HEAD_W:] - lam * (a1[:, :HEAD_W] / a1[:, HEAD_W:])
    o_ref[...] = _subln_gate(o, sub_ref[...], z_ref[...], lam_init).astype(BF16)


def _attn_prompt(p16, slopes, lam_rows, sub_row, bsz, t, lam_init, tq):
    nq = t // tq
    kern = functools.partial(_attn_prompt_kernel, tq=tq, kw=ATTN_KW, lam_init=lam_init)
    return pl.pallas_call(
        kern,
        grid=(bsz, HEADS, nq),
        in_specs=[
            pl.BlockSpec(memory_space=pltpu.SMEM),
            pl.BlockSpec((8, HEAD_W), lambda b, h, i: (0, 0)),
            pl.BlockSpec((1, HEAD_W), lambda b, h, i: (0, 0)),
            pl.BlockSpec((tq, HEAD_W), lambda b, h, i: (b * nq + i, OFF_AQ // HEAD_W + h)),
            pl.BlockSpec((t, HEAD_W), lambda b, h, i: (b, OFF_AK // HEAD_W + h)),
            pl.BlockSpec((t, HEAD_W), lambda b, h, i: (b, OFF_AV // HEAD_W + h)),
            pl.BlockSpec((tq, HEAD_W), lambda b, h, i: (b * nq + i, OFF_AZ // HEAD_W + h)),
        ],
        out_specs=pl.BlockSpec((tq, HEAD_W), lambda b, h, i: (b * nq + i, h)),
        out_shape=jax.ShapeDtypeStruct((bsz * t, HEADS * HEAD_W), BF16),
        scratch_shapes=[pltpu.VMEM((2, tq, HEAD_W), F32), pltpu.VMEM((2, tq, 2 * HEAD_W), F32)],
        compiler_params=_params("parallel", "parallel", "arbitrary"),
        name="attn_prompt",
    )(slopes, lam_rows, sub_row, p16, p16, p16, p16)


def _attn_sample_kernel(pt_ref, sloper_ref, lam_ref, sub_ref, q_ref, kn_ref, vn_ref, z_ref, *rest,
                        n_pages, page, dt, lam_init):
    del pt_ref
    kc_refs, vc_refs, o_ref = rest[:n_pages], rest[n_pages:2 * n_pages], rest[2 * n_pages]
    rows = HEADS * 2 * dt
    grp = 2 * dt
    past = n_pages * page

    q = q_ref[0].astype(F32)
    qt = jnp.tile(q, (HEADS * 2, 1))
    r = lax.broadcasted_iota(jnp.int32, (rows, D_MODEL), 0)
    c = lax.broadcasted_iota(jnp.int32, (rows, D_MODEL), 1)
    qf = jnp.where(r // dt == c // QK_DIM, qt, 0.0)
    qbd = qf.astype(BF16)
    slope_col = sloper_ref[:, 0:1] * LOG2E

    s = jnp.concatenate([jnp.dot(qbd, kc[...].astype(BF16), preferred_element_type=F32) for kc in kc_refs], axis=1)
    pos = lax.broadcasted_iota(jnp.int32, (1, past), 1)
    s = s + slope_col * (pos - past).astype(F32)

    kn = kn_ref[0].astype(F32)
    vn = vn_ref[0].astype(F32)
    rq = lax.broadcasted_iota(jnp.int32, (rows, 1), 0) % dt
    s_new = []
    for j in range(dt):
        sj = jnp.sum(qf * kn[j:j + 1, :], axis=1, keepdims=True) + slope_col * float(j)
        s_new.append(jnp.where(rq >= j, sj, -jnp.inf))

    m = jnp.max(s, axis=1, keepdims=True)
    for j in range(dt):
        m = jnp.maximum(m, s_new[j])
    p = jnp.exp2(s - m)
    p_new = [jnp.exp2(sj - m) for sj in s_new]
    l = jnp.sum(p, axis=1, keepdims=True)
    for pj in p_new:
        l = l + pj
    pb = p.astype(BF16)
    inv_l = 1.0 / l
    lam = _lam_from_rows(lam_ref, lam_init)
    zz = z_ref[0]
    sub_row = sub_ref[...]
    for h in range(HEADS):
        sl = slice(h * grp, (h + 1) * grp)
        cs = slice(h * HEAD_W, (h + 1) * HEAD_W)
        vh = jnp.concatenate([vc[pl.ds(h, page, stride=HEADS), :].astype(BF16) for vc in vc_refs], axis=0)
        acc = jnp.dot(pb[sl], vh, preferred_element_type=F32)
        for j in range(dt):
            acc = acc + p_new[j][sl] * vn[j:j + 1, cs]
        on = acc * inv_l[sl]
        o = on[0:dt] - lam * on[dt:grp]
        o_ref[0, :, cs] = _subln_gate(o, sub_row, zz[:, cs], lam_init).astype(BF16)


def _attn_sample(p16s, page_table, slope_rows, lam_rows, sub_row, cache_kt, cache_vr, layer, lam_init):
    db, dt, _ = p16s.shape
    n_pages = page_table.shape[1]
    page = cache_kt.shape[-1]
    rows = HEADS * 2 * dt
    kern = functools.partial(_attn_sample_kernel, n_pages=n_pages, page=page, dt=dt, lam_init=lam_init)
    blk = lambda c: pl.BlockSpec((1, dt, D_MODEL), lambda b, pt: (b, 0, c))

    def cache_spec(g):
        return pl.BlockSpec((None, None, HEADS * HEAD_W, page), lambda b, pt: (layer, pt[b * n_pages + g], 0, 0))

    grid_spec = pltpu.PrefetchScalarGridSpec(
        num_scalar_prefetch=1,
        grid=(db,),
        in_specs=[
            pl.BlockSpec((rows, HEAD_W), lambda b, pt: (0, 0)),
            pl.BlockSpec((8, HEAD_W), lambda b, pt: (0, 0)),
            pl.BlockSpec((1, HEAD_W), lambda b, pt: (0, 0)),
            blk(OFF_AQ // D_MODEL), blk(OFF_AK // D_MODEL), blk(OFF_AV // D_MODEL), blk(OFF_AZ // D_MODEL),
        ] + [cache_spec(g) for g in range(n_pages)] * 2,
        out_specs=pl.BlockSpec((1, dt, D_MODEL), lambda b, pt: (b, 0, 0)),
    )
    return pl.pallas_call(
        kern,
        grid_spec=grid_spec,
        out_shape=jax.ShapeDtypeStruct((db, dt, D_MODEL), BF16),
        compiler_params=_params("parallel"),
        name="attn_sample",
    )(page_table.reshape(-1), slope_rows, lam_rows, sub_row, p16s, p16s, p16s, p16s,
      *([cache_kt] * n_pages), *([cache_vr] * n_pages))


def _gates(ba, alog_row, dtb_row):
    beta = _sigmoid(ba)
    g = -jnp.exp(alog_row) * _softplus(ba + dtb_row)
    return beta, g


def _lane_col(x, idx):
    lane = lax.broadcasted_iota(jnp.int32, x.shape, 1)
    return jnp.sum(jnp.where(lane == idx, x, 0.0), axis=1, keepdims=True)


def _l2n(x):
    return x * lax.rsqrt(jnp.sum(x * x, axis=1, keepdims=True) + RMS_EPS)


def _conv_silu(xb_ref, part, x_rows, cw_ref, n_rows):
    xb_ref[part, 8:8 + x_rows.shape[0], :] = x_rows
    acc = xb_ref[part, 5:5 + n_rows, :] * cw_ref[0:1, :]
    for j in range(1, CONV_W):
        acc = acc + xb_ref[part, 5 + j:5 + j + n_rows, :] * cw_ref[j:j + 1, :]
    return _silu(acc)


def _gdn_prompt_kernel(xq_ref, xk_ref, xv_ref, z_ref, ba_ref, cwq_ref, cwk_ref, cwv_ref, csq_ref, csk_ref, csv_ref,
                       s0_ref, alog_ref, dtb_ref, norm_ref, o_ref, sout_ref, st_ref, xb_ref, *, chunk, nc, hb):
    hg = pl.program_id(1)
    t = pl.program_id(2)
    nt = pl.num_programs(2)
    c = chunk
    rows = nc * c
    w = hb * HEAD_W

    @pl.when(t == 0)
    def _():
        st_ref[...] = s0_ref[0]
        for part, cs_ref in enumerate((csq_ref, csk_ref, csv_ref)):
            xb_ref[part, 0:8, :] = jnp.zeros((8, w), F32)
            xb_ref[part, 5:8, :] = cs_ref[0]

    acts = [_conv_silu(xb_ref, part, x_ref[0], cw_ref, rows)
            for part, (x_ref, cw_ref) in enumerate(((xq_ref, cwq_ref), (xk_ref, cwk_ref), (xv_ref, cwv_ref)))]
    tails = [xb_ref[part, rows:rows + 8, :] for part in range(3)]
    zz = z_ref[0]
    norm_row = norm_ref[...]
    states = [st_ref[hh] for hh in range(hb)]

    beta_all, g_all = _gates(ba_ref[0], alog_ref[...], dtb_ref[...])
    row = lax.broadcasted_iota(jnp.int32, (c, BA_W), 0)
    ri = lax.broadcasted_iota(jnp.int32, (c, c), 0)
    ci = lax.broadcasted_iota(jnp.int32, (c, c), 1)
    causal = ci <= ri
    eye = ci == ri
    ident = jnp.where(eye, 1.0, 0.0)
    n_lvl = int(math.log2(c))
    same = [(ri >> k) == (ci >> k) for k in range(n_lvl + 1)]
    lvl_mask = [jnp.where(same[k], 0.0, jnp.where(same[k + 1], 1.0, 0.0)) * jnp.where(ci < ri, 1.0, 0.0)
                for k in range(n_lvl)]
    lvl_mask_b = [m.astype(BF16) for m in lvl_mask]

    pre = []
    for ch in range(nc):
        rs = slice(ch * c, (ch + 1) * c)
        gc_blk = g_all[rs]
        sh = 1
        while sh < c:
            gc_blk = gc_blk + jnp.where(row >= sh, pltpu.roll(gc_blk, sh, 0), 0.0)
            sh *= 2
        for hh in range(hb):
            hidx = hg * hb + hh
            bt = _lane_col(beta_all[rs], hidx)
            gcol = _lane_col(gc_blk, HEADS + hidx)
            sl = slice(hh * HEAD_W, (hh + 1) * HEAD_W)
            qn = _l2n(acts[0][rs, sl]) * (HEAD_W ** -0.5)
            kn = _l2n(acts[1][rs, sl])
            gcb = jnp.broadcast_to(gcol, (c, c))
            grow = jnp.sum(jnp.where(eye, gcb, 0.0), axis=0, keepdims=True)
            glast = gcol[c - 1:c, :]
            eg = jnp.exp(gcol)
            kb = kn * bt
            knt = kn.T
            pre.append(dict(qn=qn, knt=knt.astype(BF16), kb=kb,
                            rhs=jnp.concatenate([acts[2][rs, sl] * bt, kb * eg], axis=1),
                            dec=jnp.where(causal, jnp.exp(jnp.minimum(gcb - grow, 0.0)), 0.0),
                            qd=qn * eg, kdt=knt * jnp.exp(glast - grow), eglast=jnp.exp(glast)))
    for pr in pre:
        pr["lm"] = _mm(pr["kb"], pr["knt"]) * pr["dec"]
    for pr in pre:
        pr["qk"] = _mm(pr["qn"], pr["knt"]) * pr["dec"]
    for pr in pre:
        pr["minv"] = ident - lvl_mask[0] * pr["lm"]
        pr["lmb"] = pr["lm"].astype(BF16)
    for k in range(1, n_lvl):
        for pr in pre:
            pr["mb"] = pr["minv"].astype(BF16)
            pr["tmp"] = _mm(pr["mb"], lvl_mask_b[k] * pr["lmb"])
        for pr in pre:
            pr["minv"] = pr["minv"] - _mm(pr["tmp"], pr["mb"])
    for pr in pre:
        pr["sol"] = _mm(pr["minv"], pr["rhs"])

    outs = [None] * (nc * hb)
    for ch in range(nc):
        prs = pre[ch * hb:(ch + 1) * hb]
        vnew = [pr["sol"][:, :HEAD_W] - _mm(pr["sol"][:, HEAD_W:], states[hh]) for hh, pr in enumerate(prs)]
        o_s = [_mm(pr["qd"], states[hh]) for hh, pr in enumerate(prs)]
        o_v = [_mm(pr["qk"], vnew[hh]) for hh, pr in enumerate(prs)]
        upd = [_mm(pr["kdt"], vnew[hh]) for hh, pr in enumerate(prs)]
        for hh, pr in enumerate(prs):
            states[hh] = states[hh] * pr["eglast"] + upd[hh]
            o = o_s[hh] + o_v[hh]
            on = o * lax.rsqrt(jnp.mean(o * o, axis=1, keepdims=True) + RMS_EPS) * norm_row
            sl = slice(hh * HEAD_W, (hh + 1) * HEAD_W)
            outs[ch * hb + hh] = (on * _silu(zz[ch * c:(ch + 1) * c, sl].astype(F32))).astype(BF16)

    for ch in range(nc):
        for hh in range(hb):
            o_ref[0, ch * c:(ch + 1) * c, hh * HEAD_W:(hh + 1) * HEAD_W] = outs[ch * hb + hh]
    for hh in range(hb):
        st_ref[hh] = states[hh]
    for part in range(3):
        xb_ref[part, 0:8, :] = tails[part]

    @pl.when(t == nt - 1)
    def _():
        sout_ref[0] = st_ref[...]


def _gdn_prompt(p32, p16, pba, conv_w_l, conv_state, s0, alog_row, dtb_row, norm_row, chunk, nc, hb):
    bsz, t, _ = p32.shape
    w = hb * HEAD_W
    rows = nc * chunk
    xs = lambda off: pl.BlockSpec((1, rows, w), lambda b, g, i: (b, i, off // w + g))
    cws = lambda part: pl.BlockSpec((CONV_W, w), lambda b, g, i: (0, part * (D_MODEL // w) + g))
    css = lambda part: pl.BlockSpec((1, CONV_W - 1, w), lambda b, g, i: (b, 0, part * (D_MODEL // w) + g))
    row = pl.BlockSpec((1, HEAD_W), lambda b, g, i: (0, 0))
    kern = functools.partial(_gdn_prompt_kernel, chunk=chunk, nc=nc, hb=hb)
    return pl.pallas_call(
        kern,
        grid=(bsz, HEADS // hb, t // rows),
        in_specs=[
            xs(OFF_BQ), xs(OFF_BK), xs(OFF_BV), xs(OFF_BZ),
            pl.BlockSpec((1, rows, BA_W), lambda b, g, i: (b, i, 0)),
            cws(0), cws(1), cws(2), css(0), css(1), css(2),
            pl.BlockSpec((1, hb, HEAD_W, HEAD_W), lambda b, g, i: (b, g, 0, 0)),
            row, row, row,
        ],
        out_specs=[
            pl.BlockSpec((1, rows, w), lambda b, g, i: (b, i, g)),
            pl.BlockSpec((1, hb, HEAD_W, HEAD_W), lambda b, g, i: (b, g, 0, 0)),
        ],
        out_shape=[
            jax.ShapeDtypeStruct((bsz, t, HEADS * HEAD_W), BF16),
            jax.ShapeDtypeStruct((bsz, HEADS, HEAD_W, HEAD_W), F32),
        ],
        scratch_shapes=[pltpu.VMEM((hb, HEAD_W, HEAD_W), F32), pltpu.VMEM((3, rows + 8, w), F32)],
        compiler_params=_params("parallel", "parallel", "arbitrary"),
        name="gdn_prompt",
    )(p32, p32, p32, p16, pba, conv_w_l, conv_w_l, conv_w_l, conv_state, conv_state, conv_state, s0,
      alog_row, dtb_row, norm_row)


def _gdn_sample_kernel(xq_ref, xk_ref, xv_ref, z_ref, ba_ref, cw_ref, cs_ref, s0_ref, alog_ref, dtb_ref, norm_ref,
                       o_ref, sout_ref, xb_ref, *, dt):
    w = HEADS * HEAD_W
    acts = []
    for part, x_ref in enumerate((xq_ref, xk_ref, xv_ref)):
        xb_ref[part, 0:8, :] = jnp.zeros((8, w), F32)
        xb_ref[part, 8:16, :] = jnp.zeros((8, w), F32)
        xb_ref[part, 5:8, :] = cs_ref[0][:, part * w:(part + 1) * w]
        acts.append(_conv_silu(xb_ref, part, x_ref[0], cw_ref.at[:, part * w:(part + 1) * w], 8))

    beta_blk, g_blk = _gates(ba_ref[0], alog_ref[...], dtb_ref[...])
    rowi = lax.broadcasted_iota(jnp.int32, (8, HEAD_W), 0)

    zz_all = z_ref[0]
    norm_row = norm_ref[...]
    hd = []
    for h in range(HEADS):
        sl = slice(h * HEAD_W, (h + 1) * HEAD_W)
        qn = _l2n(acts[0][:, sl]) * (HEAD_W ** -0.5)
        kn = _l2n(acts[1][:, sl])
        s0 = s0_ref[0, h]
        kq = jnp.concatenate([kn, qn], axis=0)
        hd.append(dict(kn=kn, kq=kq, s0=s0, v=acts[2][:, sl], ks0=_mm(kq, s0)))

    for h, e in enumerate(hd):
        bt = _lane_col(beta_blk, h)
        g = _lane_col(g_blk, HEADS + h)
        kn, kq, ks0, v = e["kn"], e["kq"], e["ks0"], e["v"]
        dots = [jnp.sum(kq * kn[j:j + 1, :], axis=1, keepdims=True) for j in range(dt)]
        gc = [g[0:1, :]]
        for tt in range(1, dt):
            gc.append(gc[-1] + g[tt:tt + 1, :])
        d = []
        outs = []
        for tt in range(dt):
            if tt == 0:
                r = ks0[0:1, :]
            else:
                r = jnp.exp(gc[tt - 1]) * ks0[tt:tt + 1, :]
                for j in range(tt):
                    r = r + (jnp.exp(gc[tt - 1] - gc[j]) * dots[j][tt:tt + 1, :]) * d[j]
            d_t = bt[tt:tt + 1, :] * (v[tt:tt + 1, :] - jnp.exp(g[tt:tt + 1, :]) * r)
            d.append(d_t)
            o_t = jnp.exp(gc[tt]) * ks0[8 + tt:9 + tt, :]
            for j in range(tt + 1):
                o_t = o_t + (jnp.exp(gc[tt] - gc[j]) * dots[j][8 + tt:9 + tt, :]) * d[j]
            outs.append(o_t)
        dm = jnp.zeros((8, HEAD_W), F32)
        kd = jnp.zeros((8, HEAD_W), F32)
        om = jnp.zeros((8, HEAD_W), F32)
        for tt in range(dt):
            dm = jnp.where(rowi == tt, d[tt], dm)
            kd = jnp.where(rowi == tt, jnp.exp(gc[dt - 1] - gc[tt]) * kn[tt:tt + 1, :], kd)
            om = jnp.where(rowi == tt, outs[tt], om)
        e.update(dm=dm, kd=kd, om=om, decay=jnp.exp(gc[dt - 1]))

    upd = [_mm_tn(e["kd"], e["dm"]) for e in hd]
    for h, e in enumerate(hd):
        sl = slice(h * HEAD_W, (h + 1) * HEAD_W)
        sout_ref[0, h] = e["s0"] * e["decay"] + upd[h]
        om = e["om"]
        on = om * lax.rsqrt(jnp.mean(om * om, axis=1, keepdims=True) + RMS_EPS) * norm_row
        o_ref[0, :, sl] = (on[0:dt] * _silu(zz_all[:, sl].astype(F32))).astype(BF16)


def _gdn_sample(p32s, p16s, pbas, conv_w_l, conv_state, s0, alog_row, dtb_row, norm_row):
    db, dt, _ = p32s.shape
    w = HEADS * HEAD_W
    xs = lambda off: pl.BlockSpec((1, dt, w), lambda b: (b, 0, off // w))
    row = pl.BlockSpec((1, HEAD_W), lambda b: (0, 0))
    st = pl.BlockSpec((1, HEADS, HEAD_W, HEAD_W), lambda b: (b, 0, 0, 0))
    kern = functools.partial(_gdn_sample_kernel, dt=dt)
    return pl.pallas_call(
        kern,
        grid=(db,),
        in_specs=[
            xs(OFF_BQ), xs(OFF_BK), xs(OFF_BV), xs(OFF_BZ),
            pl.BlockSpec((1, dt, BA_W), lambda b: (b, 0, 0)),
            pl.BlockSpec((CONV_W, 3 * w), lambda b: (0, 0)),
            pl.BlockSpec((1, CONV_W - 1, 3 * w), lambda b: (b, 0, 0)),
            st, row, row, row,
        ],
        out_specs=[pl.BlockSpec((1, dt, w), lambda b: (b, 0, 0)), st],
        out_shape=[
            jax.ShapeDtypeStruct((db, dt, w), BF16),
            jax.ShapeDtypeStruct((db, HEADS, HEAD_W, HEAD_W), F32),
        ],
        scratch_shapes=[pltpu.VMEM((3, 16, w), F32)],
        compiler_params=_params("parallel"),
        name="gdn_sample",
    )(p32s, p32s, p32s, p16s, pbas, conv_w_l, conv_state, s0, alog_row, dtb_row, norm_row)


def _merge_kernel(oa_ref, ob_ref, ga_ref, gb_ref, x_ref, gate_ref, wpa_ref, wpb_ref, wo_ref, lng_ref, lnb_ref, o_ref,
                  *, alpha):
    ya = jnp.dot(oa_ref[...], wpa_ref[...], preferred_element_type=F32)
    yb = jnp.dot(ob_ref[...], wpb_ref[...], preferred_element_type=F32)
    y = _sigmoid(ga_ref[...].astype(F32)) * ya + _sigmoid(gb_ref[...].astype(F32)) * yb
    out = jnp.dot(y.astype(BF16), wo_ref[...], preferred_element_type=F32)
    zz = alpha * x_ref[...] + gate_ref[0] * out
    zc = zz - jnp.mean(zz, axis=1, keepdims=True)
    var = jnp.mean(zc * zc, axis=1, keepdims=True)
    o_ref[...] = zc * lax.rsqrt(var + LN_EPS) * lng_ref[...] + lnb_ref[...]


def _merge(oa, ob, p16, x2d, gate, wpa, wpb, wo, lng, lnb, alpha, tm):
    r, d = x2d.shape
    g, sr, _ = gate.shape
    rows_per_group = r // g
    act = pl.BlockSpec((tm, d), lambda i: (i, 0))
    wsp = pl.BlockSpec((d, d), lambda i: (0, 0))
    vec = pl.BlockSpec((1, d), lambda i: (0, 0))
    kern = functools.partial(_merge_kernel, alpha=alpha)
    return pl.pallas_call(
        kern,
        grid=(r // tm,),
        in_specs=[
            act, act,
            pl.BlockSpec((tm, d), lambda i: (i, OFF_GA // d)),
            pl.BlockSpec((tm, d), lambda i: (i, OFF_GB // d)),
            act,
            pl.BlockSpec((1, sr, d), lambda i: ((i * tm) // rows_per_group, 0, 0)),
            wsp, wsp, wsp, vec, vec,
        ],
        out_specs=act,
        out_shape=jax.ShapeDtypeStruct((r, d), F32),
        compiler_params=_params("parallel"),
        name="merge",
    )(oa, ob, p16, p16, x2d, gate, wpa, wpb, wo, lng, lnb)


def _reorder_w_in(w):
    n_head_cols = OFF_GA
    main = jnp.concatenate([w[:, :n_head_cols], w[:, n_head_cols + 2 * HEADS:]], axis=1)
    ba = jnp.pad(w[:, n_head_cols:n_head_cols + 2 * HEADS], ((0, 0), (0, BA_W - 2 * HEADS)))
    return main.astype(BF16), ba.astype(BF16)


def _pad_lanes(v, offset=0):
    return jnp.pad(v.astype(F32), (offset, HEAD_W - offset - v.shape[0])).reshape(1, HEAD_W)


def _kv_rows(p32, lead):
    k = p32[:, OFF_AK:OFF_AK + D_MODEL].reshape(lead + (HEADS, 2, QK_DIM))
    v = p32[:, OFF_AV:OFF_AV + D_MODEL].reshape(lead + (HEADS, HEAD_W))
    return k, v


def _layer(x, mod, layer, prm, sample_state, depth):
    bsz, t, d = x.shape
    r = bsz * t
    assert t >= CONV_W - 1
    lam_init = 0.8 - 0.6 * math.exp(-0.3 * layer)
    alpha = (2 * depth) ** 0.25
    shift, scale, gate = jnp.split(mod, 3, axis=-1)
    x2d = x.reshape(r, d)
    if sample_state is None:
        tm = min(1024, t)
        grp = lambda m: m[:, None, :]
    else:
        tm = r
        grp = lambda m: jnp.repeat(m, t, axis=0)[None]
    p32, p16, pba = _inproj(x2d, grp(scale), grp(shift), prm["w_main"][layer], prm["w_ba"][layer], tm)
    p32_3, p16_3, pba_3 = (a.reshape(bsz, t, -1) for a in (p32, p16, pba))
    sub_row = prm["subln_a"][layer].reshape(1, HEAD_W)
    gdn_rows = (prm["alog_row"][layer], prm["dtb_row"][layer], prm["gdn_norm"][layer].reshape(1, HEAD_W))
    conv_w_l = prm["conv_w"][layer]
    if sample_state is None:
        o_a = _attn_prompt(p16, prm["slopes"], prm["lam_rows"][layer], sub_row, bsz, t, lam_init, tq=min(ATTN_TQ, t))
        conv0 = jnp.zeros((bsz, CONV_W - 1, 3 * d), F32)
        s0 = jnp.zeros((bsz, HEADS, HEAD_W, HEAD_W), F32)
        chunk = min(GDN_CHUNK, t)
        nc = GDN_NC if t % (GDN_NC * chunk) == 0 else 1
        o_b, s_new = _gdn_prompt(p32_3, p16_3, pba_3, conv_w_l, conv0, s0, *gdn_rows, chunk=chunk, nc=nc, hb=GDN_HB)
        o_b = o_b.reshape(r, d)
    else:
        page_table, cache_kt, cache_vr, conv_state, s0 = sample_state
        slope_rows = jnp.broadcast_to(jnp.repeat(prm["slopes"], 2 * t)[:, None], (HEADS * 2 * t, HEAD_W))
        o_a = _attn_sample(p16_3, page_table, slope_rows, prm["lam_rows"][layer], sub_row, cache_kt, cache_vr,
                           layer, lam_init).reshape(r, d)
        o_b, s_new = _gdn_sample(p32_3, p16_3, pba_3, conv_w_l, conv_state, s0, *gdn_rows)
        o_b = o_b.reshape(r, d)
    x_new = _merge(o_a, o_b, p16, x2d, grp(gate), prm["w_proj_a"][layer], prm["w_proj_b"][layer],
                   prm["w_out"][layer], prm["ln_g"][layer].reshape(1, d), prm["ln_b"][layer].reshape(1, d),
                   alpha, tm=min(512, r))
    k_new, v_new = _kv_rows(p32, (bsz, t))
    buf_new = p32_3[:, t - (CONV_W - 1):, OFF_BQ:OFF_BQ + 3 * d]
    return x_new.reshape(bsz, t, d), k_new, v_new, s_new, buf_new


def kernel(x_prompt, x_sample, c_prompt, c_sample, cache_k, cache_v, state_gdn, state_conv, page_table, w_ada, b_ada, w_in, lambda_q1, lambda_k1, lambda_q2, lambda_k2, subln_a, conv_w, a_log, dt_bias, gdn_norm, w_proj_a, w_proj_b, w_out, ln_g, ln_b):
    depth = w_in.shape[0]
    bsz = x_prompt.shape[0]
    db = x_sample.shape[0]
    n_pool, page = cache_k.shape[1], cache_k.shape[2]

    rows = -(-(bsz + db) // 8) * 8
    c_all = jnp.pad(jnp.concatenate([c_prompt, c_sample], axis=0), ((0, rows - bsz - db), (0, 0)))
    mod = _ada(c_all, w_ada, b_ada)

    w_main, w_ba = zip(*[_reorder_w_in(w_in[l]) for l in range(depth)])
    lam_rows = [jnp.pad(jnp.stack([lambda_q1[l], lambda_k1[l], lambda_q2[l], lambda_k2[l]]).astype(F32),
                        ((0, 4), (0, HEAD_W - QK_DIM))) for l in range(depth)]
    prm = dict(
        w_main=w_main, w_ba=w_ba, lam_rows=lam_rows, subln_a=subln_a, conv_w=conv_w, gdn_norm=gdn_norm,
        slopes=2.0 ** (-8.0 * jnp.arange(1, HEADS + 1, dtype=F32) / HEADS),
        alog_row=[_pad_lanes(a_log[l], HEADS) for l in range(depth)],
        dtb_row=[_pad_lanes(dt_bias[l], HEADS) for l in range(depth)],
        w_proj_a=w_proj_a.astype(BF16), w_proj_b=w_proj_b.astype(BF16), w_out=w_out.astype(BF16),
        ln_g=ln_g, ln_b=ln_b,
    )
    cache_kt = jnp.transpose(cache_k, (0, 1, 3, 4, 5, 2)).reshape(depth, n_pool, HEADS * 2 * QK_DIM, page)
    cache_vr = cache_v.reshape(depth, n_pool, page * HEADS, HEAD_W)

    yp, ys = x_prompt, x_sample
    outs_p, outs_s = [], []
    for l in range(depth):
        yp, *rest = _layer(yp, mod[l, :bsz], l, prm, None, depth)
        outs_p.append(rest)
        state = (page_table, cache_kt, cache_vr, state_conv[l], state_gdn[l])
        ys, *rest = _layer(ys, mod[l, bsz:bsz + db], l, prm, state, depth)
        outs_s.append(rest)
    stack = lambda outs, i: jnp.stack([o[i] for o in outs])
    return (yp, ys, stack(outs_p, 0), stack(outs_p, 1), stack(outs_p, 2), stack(outs_p, 3),
            stack(outs_s, 0), stack(outs_s, 1), stack(outs_s, 2), stack(outs_s, 3))
```

```python
import functools
import math

import jax
import jax.numpy as jnp
from jax import lax
from jax.experimental import pallas as pl
from jax.experimental.pallas import tpu as pltpu

F32 = jnp.float32
BF16 = jnp.bfloat16

D_MODEL = 1024
HEADS = 8
HEAD_W = 128
QK_DIM = 64
CONV_W = 4
LN_EPS = 1e-5
RMS_EPS = 1e-6

OFF_AQ, OFF_AK, OFF_AV, OFF_AZ = 0, 1024, 2048, 3072
OFF_BQ, OFF_BK, OFF_BV, OFF_BZ = 4096, 5120, 6144, 7168
OFF_GA, OFF_GB = 8192, 9216
N_MAIN = 10240
P32_AK, P32_AV, P32_BQ, P32_BK, P32_BV = 0, 1024, 2048, 3072, 4096
P32_W = 5120
BA_W = 128

VMEM_LIMIT = 56 * 1024 * 1024
LOG2E = math.log2(math.e)
Q_PRESCALE = QK_DIM ** -0.5 * LOG2E
INPROJ_TN = 1024
ATTN_TQ = 512
ATTN_KW = 4
GDN_CHUNK = 128
GDN_NC = 2
GDN_HB = 8


def _sigmoid(x):
    return 1.0 / (1.0 + jnp.exp(-x))


def _silu(x):
    return x * _sigmoid(x)


def _softplus(x):
    return jnp.maximum(x, 0.0) + jnp.log1p(jnp.exp(-jnp.abs(x)))


def _mm(a, b):
    return jnp.dot(a.astype(BF16), b.astype(BF16), preferred_element_type=F32)


def _mm_nt(a, b):
    return lax.dot_general(a.astype(BF16), b.astype(BF16), (((1,), (1,)), ((), ())), preferred_element_type=F32)


def _mm_tn(a, b):
    return lax.dot_general(a.astype(BF16), b.astype(BF16), (((0,), (0,)), ((), ())), preferred_element_type=F32)


def _params(*sem):
    return pltpu.CompilerParams(dimension_semantics=sem, vmem_limit_bytes=VMEM_LIMIT)


def _ada_kernel(c_ref, w_ref, b_ref, o_ref):
    c = c_ref[...]
    o_ref[0] = _mm(_silu(c), w_ref[0]) + b_ref[0]


def _ada(c_all, w_ada, b_ada):
    depth, d, n = w_ada.shape
    rows = c_all.shape[0]
    tn = 1024
    return pl.pallas_call(
        _ada_kernel,
        grid=(depth, n // tn),
        in_specs=[
            pl.BlockSpec((rows, d), lambda l, j: (0, 0)),
            pl.BlockSpec((1, d, tn), lambda l, j: (l, 0, j)),
            pl.BlockSpec((1, 1, tn), lambda l, j: (l, 0, j)),
        ],
        out_specs=pl.BlockSpec((1, rows, tn), lambda l, j: (l, 0, j)),
        out_shape=jax.ShapeDtypeStruct((depth, rows, n), F32),
        compiler_params=_params("parallel", "parallel"),
        name="ada",
    )(c_all, w_ada, b_ada.reshape(depth, 1, n))


def _p32_block(j):
    return jnp.where(j <= 1, 0, jnp.where(j <= 3, 1, jnp.minimum(j - 2, 4)))


def _inproj_kernel(x_ref, sc_ref, sh_ref, w_ref, wba_ref, o32_ref, o16_ref, oba_ref, h_ref):
    @pl.when(pl.program_id(1) == 0)
    def _():
        h = (x_ref[...] * (1.0 + sc_ref[0]) + sh_ref[0]).astype(BF16)
        h_ref[...] = h
        oba_ref[...] = jnp.dot(h, wba_ref[...], preferred_element_type=F32)

    j = pl.program_id(1)
    acc = jnp.dot(h_ref[...], w_ref[...], preferred_element_type=F32)

    @pl.when((j == 1) | (j == 2) | ((j >= 4) & (j <= 6)))
    def _():
        o32_ref[...] = acc

    o16_ref[...] = (acc * jnp.where(pl.program_id(1) == 0, Q_PRESCALE, 1.0)).astype(BF16)


def _inproj(x2d, scale, shift, w_main, w_ba, tm):
    r, d = x2d.shape
    g, sr, _ = scale.shape
    tn = INPROJ_TN
    assert tn == OFF_AK - OFF_AQ
    rows_per_group = r // g
    mod_spec = pl.BlockSpec((1, sr, d), lambda i, j: ((i * tm) // rows_per_group, 0, 0))
    return pl.pallas_call(
        _inproj_kernel,
        grid=(r // tm, N_MAIN // tn),
        in_specs=[
            pl.BlockSpec((tm, d), lambda i, j: (i, 0)),
            mod_spec,
            mod_spec,
            pl.BlockSpec((d, tn), lambda i, j: (0, j)),
            pl.BlockSpec((d, BA_W), lambda i, j: (0, 0)),
        ],
        out_specs=[
            pl.BlockSpec((tm, tn), lambda i, j: (i, _p32_block(j))),
            pl.BlockSpec((tm, tn), lambda i, j: (i, j)),
            pl.BlockSpec((tm, BA_W), lambda i, j: (i, 0)),
        ],
        out_shape=[
            jax.ShapeDtypeStruct((r, P32_W), F32),
            jax.ShapeDtypeStruct((r, N_MAIN), BF16),
            jax.ShapeDtypeStruct((r, BA_W), F32),
        ],
        scratch_shapes=[pltpu.VMEM((tm, d), BF16)],
        compiler_params=_params("parallel", "arbitrary"),
        name="inproj",
    )(x2d, scale, shift, w_main, w_ba)


def _lam_from_rows(lam_ref, lam_init):
    a = jnp.sum(lam_ref[0:1, :] * lam_ref[1:2, :], axis=1, keepdims=True)
    b = jnp.sum(lam_ref[2:3, :] * lam_ref[3:4, :], axis=1, keepdims=True)
    return jnp.exp(a) - jnp.exp(b) + lam_init


def _subln_gate(o, sub_row, z, lam_init):
    o = o * lax.rsqrt(jnp.mean(o * o, axis=1, keepdims=True) + RMS_EPS) * sub_row * (1.0 - lam_init)
    return o * _silu(z.astype(F32))


ONES_ROWS = 16


def _attn_prompt_t_kernel(slope_ref, lam_ref, sub_ref, q_ref, k_ref, v_ref, z_ref, o_ref, vt_ref, m_ref, acc_ref,
                          *, tq, kw, lam_init):
    h = pl.program_id(1)
    i = pl.program_id(2)
    n_blk = v_ref.shape[0] // tq

    @pl.when(i == 0)
    def _():
        for cb in range(n_blk):
            vt_ref[cb, 0:HEAD_W, :] = v_ref[cb * tq:(cb + 1) * tq, :].astype(F32).T.astype(BF16)
            vt_ref[cb, HEAD_W:HEAD_W + ONES_ROWS, :] = jnp.ones((ONES_ROWS, tq), BF16)

    slope2 = slope_ref[h] * LOG2E
    lane = lax.broadcasted_iota(jnp.int32, (1, HEAD_W), 1)
    qf = q_ref[...].astype(F32)
    qmt = (jnp.where(lane < QK_DIM, qf, 0.0).T.astype(BF16), jnp.where(lane >= QK_DIM, qf, 0.0).T.astype(BF16))
    m_ref[...] = jnp.full(m_ref.shape, -jnp.inf, F32)
    acc_ref[...] = jnp.zeros(acc_ref.shape, F32)

    def step(blk0, nb, masked):
        width = nb * tq
        start = pl.multiple_of(blk0 * tq, tq)
        k = k_ref[pl.ds(start, width), :]
        row = lax.broadcasted_iota(jnp.int32, (width, 1), 0)
        bias = slope2 * (start - i * tq + row).astype(F32)
        ss = [jnp.dot(k, qmt[mi], preferred_element_type=F32) for mi in range(2)]
        ps, alphas = [], []
        for mi in range(2):
            s = ss[mi] + bias
            if masked:
                r = lax.broadcasted_iota(jnp.int32, (tq, tq), 0)
                c = lax.broadcasted_iota(jnp.int32, (tq, tq), 1)
                diag = jnp.where(r <= c, s[width - tq:], -jnp.inf)
                s = diag if nb == 1 else jnp.concatenate([s[:width - tq], diag], axis=0)
            m_prev = m_ref[mi, 0:1, :]
            m_new = jnp.maximum(m_prev, jnp.max(s, axis=0, keepdims=True))
            alphas.append(jnp.exp2(m_prev - m_new))
            ps.append(jnp.exp2(s - m_new).astype(BF16))
            m_ref[mi] = jnp.broadcast_to(m_new, (8, tq))
        for mi in range(2):
            upd = jnp.dot(vt_ref[blk0], ps[mi][0:tq], preferred_element_type=F32)
            for cb in range(1, nb):
                upd = upd + jnp.dot(vt_ref[blk0 + cb], ps[mi][cb * tq:(cb + 1) * tq], preferred_element_type=F32)
            acc_ref[mi] = alphas[mi] * acc_ref[mi] + upd

    def body(jj, carry):
        step(jj * kw, kw, False)
        return carry

    lax.fori_loop(0, i // kw, body, 0)
    for rem in range(kw):
        @pl.when(i % kw == rem)
        def _():
            step(i - rem, rem + 1, True)

    lam = _lam_from_rows(lam_ref, lam_init)
    a0, a1 = acc_ref[0], acc_ref[1]
    ot =a0[0:HEAD_W] / a0[HEAD_W:HEAD_W + 1] - lam * (a1[0:HEAD_W] / a1[HEAD_W:HEAD_W + 1])
    o_ref[...] = _subln_gate(ot.T, sub_ref[...], z_ref[...], lam_init).astype(BF16)


def _attn_prompt_t(p16, slopes, lam_rows, sub_row, bsz, t, lam_init, tq):
    nq = t // tq
    kern = functools.partial(_attn_prompt_t_kernel, tq=tq, kw=ATTN_KW, lam_init=lam_init)
    return pl.pallas_call(
        kern,
        grid=(bsz, HEADS, nq),
        in_specs=[
            pl.BlockSpec(memory_space=pltpu.SMEM),
            pl.BlockSpec((8, HEAD_W), lambda b, h, i: (0, 0)),
            pl.BlockSpec((1, HEAD_W), lambda b, h, i: (0, 0)),
            pl.BlockSpec((tq, HEAD_W), lambda b, h, i: (b * nq + i, OFF_AQ // HEAD_W + h)),
            pl.BlockSpec((t, HEAD_W), lambda b, h, i: (b, OFF_AK // HEAD_W + h)),
            pl.BlockSpec((t, HEAD_W), lambda b, h, i: (b, OFF_AV // HEAD_W + h)),
            pl.BlockSpec((tq, HEAD_W), lambda b, h, i: (b * nq + i, OFF_AZ // HEAD_W + h)),
        ],
        out_specs=pl.BlockSpec((tq, HEAD_W), lambda b, h, i: (b * nq + i, h)),
        out_shape=jax.ShapeDtypeStruct((bsz * t, HEADS * HEAD_W), BF16),
        scratch_shapes=[pltpu.VMEM((nq, HEAD_W + ONES_ROWS, tq), BF16), pltpu.VMEM((2, 8, tq), F32),
                        pltpu.VMEM((2, HEAD_W + ONES_ROWS, tq), F32)],
        compiler_params=_params("arbitrary", "arbitrary", "arbitrary"),
        name="attn_prompt",
    )(slopes, lam_rows, sub_row, p16, p16, p16, p16)


def _attn_sample_kernel(pt_ref, sloper_ref, lam_ref, sub_ref, q_ref, kn_ref, vn_ref, z_ref, *rest,
                        n_pages, page, dt, lam_init):
    del pt_ref
    kc_refs, vc_refs, o_ref = rest[:n_pages], rest[n_pages:2 * n_pages], rest[2 * n_pages]
    rows = HEADS * 2 * dt
    grp = 2 * dt
    past = n_pages * page

    q = q_ref[0].astype(F32)
    qt = jnp.tile(q, (HEADS * 2, 1))
    r = lax.broadcasted_iota(jnp.int32, (rows, D_MODEL), 0)
    c = lax.broadcasted_iota(jnp.int32, (rows, D_MODEL), 1)
    qf = jnp.where(r // dt == c // QK_DIM, qt, 0.0)
    qbd = qf.astype(BF16)
    slope_col = sloper_ref[:, 0:1] * LOG2E

    s = jnp.concatenate([jnp.dot(qbd, kc[...].astype(BF16), preferred_element_type=F32) for kc in kc_refs], axis=1)
    pos = lax.broadcasted_iota(jnp.int32, (1, past), 1)
    s = s + slope_col * (pos - past).astype(F32)

    kn = kn_ref[0].astype(F32)
    vn = vn_ref[0].astype(F32)
    rq = lax.broadcasted_iota(jnp.int32, (rows, 1), 0) % dt
    s_new = []
    for j in range(dt):
        sj = jnp.sum(qf * kn[j:j + 1, :], axis=1, keepdims=True) + slope_col * float(j)
        s_new.append(jnp.where(rq >= j, sj, -jnp.inf))

    m = jnp.max(s, axis=1, keepdims=True)
    for j in range(dt):
        m = jnp.maximum(m, s_new[j])
    p = jnp.exp2(s - m)
    p_new = [jnp.exp2(sj - m) for sj in s_new]
    l = jnp.sum(p, axis=1, keepdims=True)
    for pj in p_new:
        l = l + pj
    pb = p.astype(BF16)
    inv_l = 1.0 / l
    lam = _lam_from_rows(lam_ref, lam_init)
    zz = z_ref[0]
    sub_row = sub_ref[...]
    for h in range(HEADS):
        sl = slice(h * grp, (h + 1) * grp)
        cs = slice(h * HEAD_W, (h + 1) * HEAD_W)
        vh = jnp.concatenate([vc[pl.ds(h, page, stride=HEADS), :].astype(BF16) for vc in vc_refs], axis=0)
        acc = jnp.dot(pb[sl], vh, preferred_element_type=F32)
        for j in range(dt):
            acc = acc + p_new[j][sl] * vn[j:j + 1, cs]
        on = acc * inv_l[sl]
        o = on[0:dt] - lam * on[dt:grp]
        o_ref[0, :, cs] = _subln_gate(o, sub_row, zz[:, cs], lam_init).astype(BF16)


def _attn_sample(p16s, page_table, slope_rows, lam_rows, sub_row, cache_kt, cache_vr, layer, lam_init):
    db, dt, _ = p16s.shape
    n_pages = page_table.shape[1]
    page = cache_kt.shape[-1]
    rows = HEADS * 2 * dt
    kern = functools.partial(_attn_sample_kernel, n_pages=n_pages, page=page, dt=dt, lam_init=lam_init)
    blk = lambda c: pl.BlockSpec((1, dt, D_MODEL), lambda b, pt: (b, 0, c))

    def cache_spec(g):
        return pl.BlockSpec((None, None, HEADS * HEAD_W, page), lambda b, pt: (layer, pt[b * n_pages + g], 0, 0))

    grid_spec = pltpu.PrefetchScalarGridSpec(
        num_scalar_prefetch=1,
        grid=(db,),
        in_specs=[
            pl.BlockSpec((rows, HEAD_W), lambda b, pt: (0, 0)),
            pl.BlockSpec((8, HEAD_W), lambda b, pt: (0, 0)),
            pl.BlockSpec((1, HEAD_W), lambda b, pt: (0, 0)),
            blk(OFF_AQ // D_MODEL), blk(OFF_AK // D_MODEL), blk(OFF_AV // D_MODEL), blk(OFF_AZ // D_MODEL),
        ] + [cache_spec(g) for g in range(n_pages)] * 2,
        out_specs=pl.BlockSpec((1, dt, D_MODEL), lambda b, pt: (b, 0, 0)),
    )
    return pl.pallas_call(
        kern,
        grid_spec=grid_spec,
        out_shape=jax.ShapeDtypeStruct((db, dt, D_MODEL), BF16),
        compiler_params=_params("parallel"),
        name="attn_sample",
    )(page_table.reshape(-1), slope_rows, lam_rows, sub_row, p16s, p16s, p16s, p16s,
      *([cache_kt] * n_pages), *([cache_vr] * n_pages))


def _gates(ba, alog_row, dtb_row):
    beta = _sigmoid(ba)
    g = -jnp.exp(alog_row) * _softplus(ba + dtb_row)
    return beta, g


def _lane_col(x, idx):
    lane = lax.broadcasted_iota(jnp.int32, x.shape, 1)
    return jnp.sum(jnp.where(lane == idx, x, 0.0), axis=1, keepdims=True)


def _l2n(x):
    return x * lax.rsqrt(jnp.sum(x * x, axis=1, keepdims=True) + RMS_EPS)


def _conv_silu(xb_ref, part, x_rows, cw_ref, n_rows):
    xb_ref[part, 8:8 + x_rows.shape[0], :] = x_rows
    acc = xb_ref[part, 5:5 + n_rows, :] * cw_ref[0:1, :]
    for j in range(1, CONV_W):
        acc = acc + xb_ref[part, 5 + j:5 + j + n_rows, :] * cw_ref[j:j + 1, :]
    return _silu(acc)


def _gdn_prompt_kernel(xq_ref, xk_ref, xv_ref, z_ref, ba_ref, cwq_ref, cwk_ref, cwv_ref, csq_ref, csk_ref, csv_ref,
                       s0_ref, alog_ref, dtb_ref, norm_ref, o_ref, sout_ref, st_ref, xb_ref, *, chunk, nc, hb):
    hg = pl.program_id(1)
    t = pl.program_id(2)
    nt = pl.num_programs(2)
    c = chunk
    rows = nc * c
    w = hb * HEAD_W

    @pl.when(t == 0)
    def _():
        st_ref[...] = s0_ref[0]
        for part, cs_ref in enumerate((csq_ref, csk_ref, csv_ref)):
            xb_ref[part, 0:8, :] = jnp.zeros((8, w), F32)
            xb_ref[part, 5:8, :] = cs_ref[0]

    acts = [_conv_silu(xb_ref, part, x_ref[0], cw_ref, rows)
            for part, (x_ref, cw_ref) in enumerate(((xq_ref, cwq_ref), (xk_ref, cwk_ref), (xv_ref, cwv_ref)))]
    tails = [xb_ref[part, rows:rows + 8, :] for part in range(3)]
    zz = z_ref[0]
    norm_row = norm_ref[...]
    states = [st_ref[hh] for hh in range(hb)]

    beta_all, g_all = _gates(ba_ref[0], alog_ref[...], dtb_ref[...])
    row = lax.broadcasted_iota(jnp.int32, (c, BA_W), 0)
    ri = lax.broadcasted_iota(jnp.int32, (c, c), 0)
    ci = lax.broadcasted_iota(jnp.int32, (c, c), 1)
    causal = ci <= ri
    eye = ci == ri
    ident = jnp.where(eye, 1.0, 0.0)
    n_lvl = int(math.log2(c))
    same = [(ri >> k) == (ci >> k) for k in range(n_lvl + 1)]
    lvl_mask = [jnp.where(same[k], 0.0, jnp.where(same[k + 1], 1.0, 0.0)) * jnp.where(ci < ri, 1.0, 0.0)
                for k in range(n_lvl)]
    lvl_mask_b = [m.astype(BF16) for m in lvl_mask]

    pre = []
    for ch in range(nc):
        rs = slice(ch * c, (ch + 1) * c)
        gc_blk = g_all[rs]
        sh = 1
        while sh < c:
            gc_blk = gc_blk + jnp.where(row >= sh, pltpu.roll(gc_blk, sh, 0), 0.0)
            sh *= 2
        for hh in range(hb):
            hidx = hg * hb + hh
            bt = _lane_col(beta_all[rs], hidx)
            gcol = _lane_col(gc_blk, HEADS + hidx)
            sl = slice(hh * HEAD_W, (hh + 1) * HEAD_W)
            qn = _l2n(acts[0][rs, sl]) * (HEAD_W ** -0.5)
            kn = _l2n(acts[1][rs, sl])
            gcb = jnp.broadcast_to(gcol, (c, c))
            grow = jnp.sum(jnp.where(eye, gcb, 0.0), axis=0, keepdims=True)
            glast = gcol[c - 1:c, :]
            eg = jnp.exp(gcol)
            kb = kn * bt
            knt = kn.T
            pre.append(dict(qn=qn, knt=knt.astype(BF16), kb=kb,
                            rhs=jnp.concatenate([acts[2][rs, sl] * bt, kb * eg], axis=1),
                            dec=jnp.where(causal, jnp.exp(jnp.minimum(gcb - grow, 0.0)), 0.0),
                            qd=qn * eg, kdt=knt * jnp.exp(glast - grow), eglast=jnp.exp(glast)))
    for pr in pre:
        pr["lm"] = _mm(pr["kb"], pr["knt"]) * pr["dec"]
    for pr in pre:
        pr["qk"] = _mm(pr["qn"], pr["knt"]) * pr["dec"]
    for pr in pre:
        pr["minv"] = ident - lvl_mask[0] * pr["lm"]
        pr["lmb"] = pr["lm"].astype(BF16)
    for k in range(1, n_lvl):
        for pr in pre:
            pr["mb"] = pr["minv"].astype(BF16)
            pr["tmp"] = _mm(pr["mb"], lvl_mask_b[k] * pr["lmb"])
        for pr in pre:
            pr["minv"] = pr["minv"] - _mm(pr["tmp"], pr["mb"])
    for pr in pre:
        pr["sol"] = _mm(pr["minv"], pr["rhs"])

    outs = [None] * (nc * hb)
    for ch in range(nc):
        prs = pre[ch * hb:(ch + 1) * hb]
        vnew = [pr["sol"][:, :HEAD_W] - _mm(pr["sol"][:, HEAD_W:], states[hh]) for hh, pr in enumerate(prs)]
        o_s = [_mm(pr["qd"], states[hh]) for hh, pr in enumerate(prs)]
        o_v = [_mm(pr["qk"], vnew[hh]) for hh, pr in enumerate(prs)]
        upd = [_mm(pr["kdt"], vnew[hh]) for hh, pr in enumerate(prs)]
        for hh, pr in enumerate(prs):
            states[hh] = states[hh] * pr["eglast"] + upd[hh]
            o = o_s[hh] + o_v[hh]
            on = o * lax.rsqrt(jnp.mean(o * o, axis=1, keepdims=True) + RMS_EPS) * norm_row
            sl = slice(hh * HEAD_W, (hh + 1) * HEAD_W)
            outs[ch * hb + hh] = (on * _silu(zz[ch * c:(ch + 1) * c, sl].astype(F32))).astype(BF16)

    for ch in range(nc):
        for hh in range(hb):
            o_ref[0, ch * c:(ch + 1) * c, hh * HEAD_W:(hh + 1) * HEAD_W] = outs[ch * hb + hh]
    for hh in range(hb):
        st_ref[hh] = states[hh]
    for part in range(3):
        xb_ref[part, 0:8, :] = tails[part]

    @pl.when(t == nt - 1)
    def _():
        sout_ref[0] = st_ref[...]


def _gdn_prompt(p32, p16, pba, conv_w_l, conv_state, s0, alog_row, dtb_row, norm_row, chunk, nc, hb):
    bsz, t, _ = p32.shape
    w = hb * HEAD_W
    rows = nc * chunk
    xs = lambda off: pl.BlockSpec((1, rows, w), lambda b, g, i: (b, i, off // w + g))
    cws = lambda part: pl.BlockSpec((CONV_W, w), lambda b, g, i: (0, part * (D_MODEL // w) + g))
    css = lambda part: pl.BlockSpec((1, CONV_W - 1, w), lambda b, g, i: (b, 0, part * (D_MODEL // w) + g))
    row = pl.BlockSpec((1, HEAD_W), lambda b, g, i: (0, 0))
    kern = functools.partial(_gdn_prompt_kernel, chunk=chunk, nc=nc, hb=hb)
    return pl.pallas_call(
        kern,
        grid=(bsz, HEADS // hb, t // rows),
        in_specs=[
            xs(P32_BQ), xs(P32_BK), xs(P32_BV), xs(OFF_BZ),
            pl.BlockSpec((1, rows, BA_W), lambda b, g, i: (b, i, 0)),
            cws(0), cws(1), cws(2), css(0), css(1), css(2),
            pl.BlockSpec((1, hb, HEAD_W, HEAD_W), lambda b, g, i: (b, g, 0, 0)),
            row, row, row,
        ],
        out_specs=[
            pl.BlockSpec((1, rows, w), lambda b, g, i: (b, i, g)),
            pl.BlockSpec((1, hb, HEAD_W, HEAD_W), lambda b, g, i: (b, g, 0, 0)),
        ],
        out_shape=[
            jax.ShapeDtypeStruct((bsz, t, HEADS * HEAD_W), BF16),
            jax.ShapeDtypeStruct((bsz, HEADS, HEAD_W, HEAD_W), F32),
        ],
        scratch_shapes=[pltpu.VMEM((hb, HEAD_W, HEAD_W), F32), pltpu.VMEM((3, rows + 8, w), F32)],
        compiler_params=_params("parallel", "parallel", "arbitrary"),
        name="gdn_prompt",
    )(p32, p32, p32, p16, pba, conv_w_l, conv_w_l, conv_w_l, conv_state, conv_state, conv_state, s0,
      alog_row, dtb_row, norm_row)


def _gdn_sample_kernel(xq_ref, xk_ref, xv_ref, z_ref, ba_ref, cw_ref, cs_ref, s0_ref, alog_ref, dtb_ref, norm_ref,
                       o_ref, sout_ref, xb_ref, *, dt):
    w = HEADS * HEAD_W
    acts = []
    for part, x_ref in enumerate((xq_ref, xk_ref, xv_ref)):
        xb_ref[part, 0:8, :] = jnp.zeros((8, w), F32)
        xb_ref[part, 8:16, :] = jnp.zeros((8, w), F32)
        xb_ref[part, 5:8, :] = cs_ref[0][:, part * w:(part + 1) * w]
        acts.append(_conv_silu(xb_ref, part, x_ref[0], cw_ref.at[:, part * w:(part + 1) * w], 8))

    beta_blk, g_blk = _gates(ba_ref[0], alog_ref[...], dtb_ref[...])
    rowi = lax.broadcasted_iota(jnp.int32, (8, HEAD_W), 0)

    zz_all = z_ref[0]
    norm_row = norm_ref[...]
    hd = []
    for h in range(HEADS):
        sl = slice(h * HEAD_W, (h + 1) * HEAD_W)
        qn = _l2n(acts[0][:, sl]) * (HEAD_W ** -0.5)
        kn = _l2n(acts[1][:, sl])
        s0 = s0_ref[0, h]
        kq = jnp.concatenate([kn, qn], axis=0)
        hd.append(dict(kn=kn, kq=kq, s0=s0, v=acts[2][:, sl], ks0=_mm(kq, s0)))

    for h, e in enumerate(hd):
        bt = _lane_col(beta_blk, h)
        g = _lane_col(g_blk, HEADS + h)
        kn, kq, ks0, v = e["kn"], e["kq"], e["ks0"], e["v"]
        dots = [jnp.sum(kq * kn[j:j + 1, :], axis=1, keepdims=True) for j in range(dt)]
        gc = [g[0:1, :]]
        for tt in range(1, dt):
            gc.append(gc[-1] + g[tt:tt + 1, :])
        d = []
        outs = []
        for tt in range(dt):
            if tt == 0:
                r = ks0[0:1, :]
            else:
                r = jnp.exp(gc[tt - 1]) * ks0[tt:tt + 1, :]
                for j in range(tt):
                    r = r + (jnp.exp(gc[tt - 1] - gc[j]) * dots[j][tt:tt + 1, :]) * d[j]
            d_t = bt[tt:tt + 1, :] * (v[tt:tt + 1, :] - jnp.exp(g[tt:tt + 1, :]) * r)
            d.append(d_t)
            o_t = jnp.exp(gc[tt]) * ks0[8 + tt:9 + tt, :]
            for j in range(tt + 1):
                o_t = o_t + (jnp.exp(gc[tt] - gc[j]) * dots[j][8 + tt:9 + tt, :]) * d[j]
            outs.append(o_t)
        dm = jnp.zeros((8, HEAD_W), F32)
        kd = jnp.zeros((8, HEAD_W), F32)
        om = jnp.zeros((8, HEAD_W), F32)
        for tt in range(dt):
            dm = jnp.where(rowi == tt, d[tt], dm)
            kd = jnp.where(rowi == tt, jnp.exp(gc[dt - 1] - gc[tt]) * kn[tt:tt + 1, :], kd)
            om = jnp.where(rowi == tt, outs[tt], om)
        e.update(dm=dm, kd=kd, om=om, decay=jnp.exp(gc[dt - 1]))

    upd = [_mm_tn(e["kd"], e["dm"]) for e in hd]
    for h, e in enumerate(hd):
        sl = slice(h * HEAD_W, (h + 1) * HEAD_W)
        sout_ref[0, h] = e["s0"] * e["decay"] + upd[h]
        om = e["om"]
        on = om * lax.rsqrt(jnp.mean(om * om, axis=1, keepdims=True) + RMS_EPS) * norm_row
        o_ref[0, :, sl] = (on[0:dt] * _silu(zz_all[:, sl].astype(F32))).astype(BF16)


def _gdn_sample(p32s, p16s, pbas, conv_w_l, conv_state, s0, alog_row, dtb_row, norm_row):
    db, dt, _ = p32s.shape
    w = HEADS * HEAD_W
    xs = lambda off: pl.BlockSpec((1, dt, w), lambda b: (b, 0, off // w))
    row = pl.BlockSpec((1, HEAD_W), lambda b: (0, 0))
    st = pl.BlockSpec((1, HEADS, HEAD_W, HEAD_W), lambda b: (b, 0, 0, 0))
    kern = functools.partial(_gdn_sample_kernel, dt=dt)
    return pl.pallas_call(
        kern,
        grid=(db,),
        in_specs=[
            xs(P32_BQ), xs(P32_BK), xs(P32_BV), xs(OFF_BZ),
            pl.BlockSpec((1, dt, BA_W), lambda b: (b, 0, 0)),
            pl.BlockSpec((CONV_W, 3 * w), lambda b: (0, 0)),
            pl.BlockSpec((1, CONV_W - 1, 3 * w), lambda b: (b, 0, 0)),
            st, row, row, row,
        ],
        out_specs=[pl.BlockSpec((1, dt, w), lambda b: (b, 0, 0)), st],
        out_shape=[
            jax.ShapeDtypeStruct((db, dt, w), BF16),
            jax.ShapeDtypeStruct((db, HEADS, HEAD_W, HEAD_W), F32),
        ],
        scratch_shapes=[pltpu.VMEM((3, 16, w), F32)],
        compiler_params=_params("parallel"),
        name="gdn_sample",
    )(p32s, p32s, p32s, p16s, pbas, conv_w_l, conv_state, s0, alog_row, dtb_row, norm_row)


def _merge_kernel(oa_ref, ob_ref, ga_ref, gb_ref, x_ref, gate_ref, wpa_ref, wpb_ref, wo_ref, lng_ref, lnb_ref, o_ref,
                  *, alpha):
    ya = jnp.dot(oa_ref[...], wpa_ref[...], preferred_element_type=F32)
    yb = jnp.dot(ob_ref[...], wpb_ref[...], preferred_element_type=F32)
    y = _sigmoid(ga_ref[...].astype(F32)) * ya + _sigmoid(gb_ref[...].astype(F32)) * yb
    out = jnp.dot(y.astype(BF16), wo_ref[...], preferred_element_type=F32)
    zz = alpha * x_ref[...] + gate_ref[0] * out
    zc = zz - jnp.mean(zz, axis=1, keepdims=True)
    var = jnp.mean(zc * zc, axis=1, keepdims=True)
    o_ref[...] = zc * lax.rsqrt(var + LN_EPS) * lng_ref[...] + lnb_ref[...]


def _merge(oa, ob, p16, x2d, gate, wpa, wpb, wo, lng, lnb, alpha, tm):
    r, d = x2d.shape
    g, sr, _ = gate.shape
    rows_per_group = r // g
    act = pl.BlockSpec((tm, d), lambda i: (i, 0))
    wsp = pl.BlockSpec((d, d), lambda i: (0, 0))
    vec = pl.BlockSpec((1, d), lambda i: (0, 0))
    kern = functools.partial(_merge_kernel, alpha=alpha)
    return pl.pallas_call(
        kern,
        grid=(r // tm,),
        in_specs=[
            act, act,
            pl.BlockSpec((tm, d), lambda i: (i, OFF_GA // d)),
            pl.BlockSpec((tm, d), lambda i: (i, OFF_GB // d)),
            act,
            pl.BlockSpec((1, sr, d), lambda i: ((i * tm) // rows_per_group, 0, 0)),
            wsp, wsp, wsp, vec, vec,
        ],
        out_specs=act,
        out_shape=jax.ShapeDtypeStruct((r, d), F32),
        compiler_params=_params("parallel"),
        name="merge",
    )(oa, ob, p16, p16, x2d, gate, wpa, wpb, wo, lng, lnb)


def _reorder_w_in(w):
    n_head_cols = OFF_GA
    main = jnp.concatenate([w[:, :n_head_cols], w[:, n_head_cols + 2 * HEADS:]], axis=1)
    ba = jnp.pad(w[:, n_head_cols:n_head_cols + 2 * HEADS], ((0, 0), (0, BA_W - 2 * HEADS)))
    return main.astype(BF16), ba.astype(BF16)


def _pad_lanes(v, offset=0):
    return jnp.pad(v.astype(F32), (offset, HEAD_W - offset - v.shape[0])).reshape(1, HEAD_W)


def _kv_rows(p32, lead):
    k = p32[:, P32_AK:P32_AK + D_MODEL].reshape(lead + (HEADS, 2, QK_DIM))
    v = p32[:, P32_AV:P32_AV + D_MODEL].reshape(lead + (HEADS, HEAD_W))
    return k, v


def _layer(x, mod, layer, prm, sample_state, depth):
    bsz, t, d = x.shape
    r = bsz * t
    assert t >= CONV_W - 1
    lam_init = 0.8 - 0.6 * math.exp(-0.3 * layer)
    alpha = (2 * depth) ** 0.25
    shift, scale, gate = jnp.split(mod, 3, axis=-1)
    x2d = x.reshape(r, d)
    if sample_state is None:
        tm = min(1024, t)
        grp = lambda m: m[:, None, :]
    else:
        tm = r
        grp = lambda m: jnp.repeat(m, t, axis=0)[None]
    p32, p16, pba = _inproj(x2d, grp(scale), grp(shift), prm["w_main"][layer], prm["w_ba"][layer], tm)
    p32_3, p16_3, pba_3 = (a.reshape(bsz, t, -1) for a in (p32, p16, pba))
    sub_row = prm["subln_a"][layer].reshape(1, HEAD_W)
    gdn_rows = (prm["alog_row"][layer], prm["dtb_row"][layer], prm["gdn_norm"][layer].reshape(1, HEAD_W))
    conv_w_l = prm["conv_w"][layer]
    if sample_state is None:
        o_a = _attn_prompt_t(p16, prm["slopes"], prm["lam_rows"][layer], sub_row, bsz, t, lam_init, tq=min(ATTN_TQ, t))
        conv0 = jnp.zeros((bsz, CONV_W - 1, 3 * d), F32)
        s0 = jnp.zeros((bsz, HEADS, HEAD_W, HEAD_W), F32)
        chunk = min(GDN_CHUNK, t)
        nc = GDN_NC if t % (GDN_NC * chunk) == 0 else 1
        o_b, s_new = _gdn_prompt(p32_3, p16_3, pba_3, conv_w_l, conv0, s0, *gdn_rows, chunk=chunk, nc=nc, hb=GDN_HB)
        o_b = o_b.reshape(r, d)
    else:
        page_table, cache_kt, cache_vr, conv_state, s0 = sample_state
        slope_rows = jnp.broadcast_to(jnp.repeat(prm["slopes"], 2 * t)[:, None], (HEADS * 2 * t, HEAD_W))
        o_a = _attn_sample(p16_3, page_table, slope_rows, prm["lam_rows"][layer], sub_row, cache_kt, cache_vr,
                           layer, lam_init).reshape(r, d)
        o_b, s_new = _gdn_sample(p32_3, p16_3, pba_3, conv_w_l, conv_state, s0, *gdn_rows)
        o_b = o_b.reshape(r, d)
    x_new = _merge(o_a, o_b, p16, x2d, grp(gate), prm["w_proj_a"][layer], prm["w_proj_b"][layer],
                   prm["w_out"][layer], prm["ln_g"][layer].reshape(1, d), prm["ln_b"][layer].reshape(1, d),
                   alpha, tm=min(512, r))
    k_new, v_new = _kv_rows(p32, (bsz, t))
    buf_new = p32_3[:, t - (CONV_W - 1):, P32_BQ:P32_BQ + 3 * d]
    return x_new.reshape(bsz, t, d), k_new, v_new, s_new, buf_new


def kernel(x_prompt, x_sample, c_prompt, c_sample, cache_k, cache_v, state_gdn, state_conv, page_table, w_ada, b_ada, w_in, lambda_q1, lambda_k1, lambda_q2, lambda_k2, subln_a, conv_w, a_log, dt_bias, gdn_norm, w_proj_a, w_proj_b, w_out, ln_g, ln_b):
    depth = w_in.shape[0]
    bsz = x_prompt.shape[0]
    db = x_sample.shape[0]
    n_pool, page = cache_k.shape[1], cache_k.shape[2]

    rows = -(-(bsz + db) // 8) * 8
    c_all = jnp.pad(jnp.concatenate([c_prompt, c_sample], axis=0), ((0, rows - bsz - db), (0, 0)))
    mod = _ada(c_all, w_ada, b_ada)

    w_main, w_ba = zip(*[_reorder_w_in(w_in[l]) for l in range(depth)])
    lam_rows = [jnp.pad(jnp.stack([lambda_q1[l], lambda_k1[l], lambda_q2[l], lambda_k2[l]]).astype(F32),
                        ((0, 4), (0, HEAD_W - QK_DIM))) for l in range(depth)]
    prm = dict(
        w_main=w_main, w_ba=w_ba, lam_rows=lam_rows, subln_a=subln_a, conv_w=conv_w, gdn_norm=gdn_norm,
        slopes=2.0 ** (-8.0 * jnp.arange(1, HEADS + 1, dtype=F32) / HEADS),
        alog_row=[_pad_lanes(a_log[l], HEADS) for l in range(depth)],
        dtb_row=[_pad_lanes(dt_bias[l], HEADS) for l in range(depth)],
        w_proj_a=w_proj_a.astype(BF16), w_proj_b=w_proj_b.astype(BF16), w_out=w_out.astype(BF16),
        ln_g=ln_g, ln_b=ln_b,
    )
    cache_kt = jnp.transpose(cache_k, (0, 1, 3, 4, 5, 2)).reshape(depth, n_pool, HEADS * 2 * QK_DIM, page)
    cache_vr = cache_v.reshape(depth, n_pool, page * HEADS, HEAD_W)

    yp, ys = x_prompt, x_sample
    outs_p, outs_s = [], []
    for l in range(depth):
        yp, *rest = _layer(yp, mod[l, :bsz], l, prm, None, depth)
        outs_p.append(rest)
        state = (page_table, cache_kt, cache_vr, state_conv[l], state_gdn[l])
        ys, *rest = _layer(ys, mod[l, bsz:bsz + db], l, prm, state, depth)
        outs_s.append(rest)
    stack = lambda outs, i: jnp.stack([o[i] for o in outs])
    return (yp, ys, stack(outs_p, 0), stack(outs_p, 1), stack(outs_p, 2), stack(outs_p, 3),
            stack(outs_s, 0), stack(outs_s, 1), stack(outs_s, 2), stack(outs_s, 3))
```
